```python
import math
import jax, jax.numpy as jnp
from jax import lax
import numpy as np


D_MODEL = 1024
BATCH = 1
SEQ = 16384
DEPTH = 2
DEC_BATCH = 8
DEC_SEQ = 2048
PAST_LEN = 128

N_MIXERS = 2
S5_GROUP = 16
S5_GROUPS = D_MODEL // S5_GROUP
S5_STATE = 64
S5_DT_MIN = 0.001
S5_DT_MAX = 0.1
POOL_WINDOWS = (2, 4, 8, 16)
POOL_GROUPS = len(POOL_WINDOWS)
POOL_CH = D_MODEL // POOL_GROUPS
D_FF = 2816
RMS_EPS = 1e-6
N_S5_LAYERS = (DEPTH + 1) // 2
N_POOL_LAYERS = DEPTH // 2

kernel_name = 'hybrid_s5_pool_macaron_encoder'


def _rms(x, g):
    xf = x.astype(jnp.float32)
    y = xf * lax.rsqrt(jnp.mean(xf * xf, axis=-1, keepdims=True) + RMS_EPS)
    return (y * g.astype(jnp.float32)).astype(x.dtype)


def _swiglu(x, wg, wu, wd):
    return (jax.nn.silu(x @ wg) * (x @ wu)) @ wd


def _ssm_combine(e1, e2):
    a1r, a1i, b1r, b1i = e1
    a2r, a2i, b2r, b2i = e2
    return (a2r * a1r - a2i * a1i,
            a2r * a1i + a2i * a1r,
            a2r * b1r - a2i * b1i + b2r,
            a2r * b1i + a2i * b1r + b2i)


def _s5_mixer(u, lam_re, lam_im, log_step, b_re, b_im, c_re, c_im, d, w_a, w_b):
    f32 = jnp.float32
    bsz, seq, _ = u.shape
    uf = u.astype(f32)
    ug = uf.reshape(bsz, seq, S5_GROUPS, S5_GROUP)
    y = uf * d.astype(f32)
    for direction in range(2):
        lr = lam_re[direction].astype(f32)
        li = lam_im[direction].astype(f32)
        step = jnp.exp(log_step[direction].astype(f32))[:, None]
        mag = jnp.exp(lr * step)
        ab_re = mag * jnp.cos(li * step)
        ab_im = mag * jnp.sin(li * step)
        den = lr * lr + li * li
        nr = ab_re - 1.0
        k_re = (nr * lr + ab_im * li) / den
        k_im = (ab_im * lr - nr * li) / den
        br = b_re[direction].astype(f32)
        bi = b_im[direction].astype(f32)
        bb_re = k_re[..., None] * br - k_im[..., None] * bi
        bb_im = k_re[..., None] * bi + k_im[..., None] * br
        bu_re = jnp.einsum('blgh,gph->blgp', ug, bb_re)
        bu_im = jnp.einsum('blgh,gph->blgp', ug, bb_im)
        a_re = jnp.broadcast_to(ab_re, bu_re.shape)
        a_im = jnp.broadcast_to(ab_im, bu_im.shape)
        _, _, s_re, s_im = lax.associative_scan(
            _ssm_combine, (a_re, a_im, bu_re, bu_im), reverse=(direction == 1), axis=1)
        y_dir = (jnp.einsum('blgp,ghp->blgh', s_re, c_re[direction].astype(f32))
                 - jnp.einsum('blgp,ghp->blgh', s_im, c_im[direction].astype(f32)))
        y = y + y_dir.reshape(bsz, seq, D_MODEL)
    g = jax.nn.gelu(y).astype(u.dtype)
    return (g @ w_a) * jax.nn.sigmoid(g @ w_b)


def _pool_mixer(u, w, scale):
    f32 = jnp.float32
    bsz, seq, _ = u.shape
    uf = u.astype(f32)
    csum = jnp.concatenate([jnp.zeros((bsz, 1, D_MODEL), f32), lax.cumsum(uf, axis=1)], axis=1)
    pos = jnp.arange(seq)
    parts = []
    for g, win in enumerate(POOL_WINDOWS):
        sl = slice(g * POOL_CH, (g + 1) * POOL_CH)
        lo = jnp.clip(pos - win // 2, 0, seq)
        hi = jnp.clip(pos + win - win // 2, 0, seq)
        cs = csum[..., sl]
        mean = (cs[:, hi] - cs[:, lo]) / (hi - lo).astype(f32)[None, :, None]
        parts.append(mean - uf[..., sl])
    p = jnp.stack(parts, axis=2)
    z = jnp.einsum('blgc,gcd->blgd', p, w.astype(f32)).reshape(bsz, seq, D_MODEL)
    return (z * scale.astype(f32)).astype(u.dtype)


def _trunk(x, norm_g, final_norm_g, ffn_w_gate, ffn_w_up, ffn_w_down,
           s5_lambda_re, s5_lambda_im, s5_log_step, s5_b_re, s5_b_im, s5_c_re, s5_c_im,
           s5_d, s5_w_glu_a, s5_w_glu_b, pool_w, pool_scale):
    for layer in range(DEPTH):
        g = norm_g[layer]
        x = x + 0.5 * _swiglu(_rms(x, g[0]), ffn_w_gate[layer, 0], ffn_w_up[layer, 0], ffn_w_down[layer, 0])
        h = _rms(x, g[1])
        j = layer // N_MIXERS
        if layer % N_MIXERS == 0:
            m = _s5_mixer(h, s5_lambda_re[j], s5_lambda_im[j], s5_log_step[j], s5_b_re[j], s5_b_im[j],
                          s5_c_re[j], s5_c_im[j], s5_d[j], s5_w_glu_a[j], s5_w_glu_b[j])
        else:
            m = _pool_mixer(h, pool_w[j], pool_scale[j])
        x = x + m
        x = x + 0.5 * _swiglu(_rms(x, g[2]), ffn_w_gate[layer, 1], ffn_w_up[layer, 1], ffn_w_down[layer, 1])
    return _rms(x, final_norm_g)


def setup_inputs(seed: int = 0) -> dict:
    key = jax.random.key(seed)
    ks = jax.random.split(key, 20)
    f32 = jnp.float32
    nrm = jax.random.normal
    lam_im_base = jnp.pi * jnp.arange(S5_STATE, dtype=f32)
    return {
        'x_prompt': nrm(ks[0], (BATCH, SEQ, D_MODEL), f32),
        'x_sample': nrm(ks[1], (DEC_BATCH, DEC_SEQ, D_MODEL), f32),
        'norm_g': 1.0 + 0.02 * nrm(ks[2], (DEPTH, 3, D_MODEL), f32),
        'final_norm_g': 1.0 + 0.02 * nrm(ks[3], (D_MODEL,), f32),
        'ffn_w_gate': nrm(ks[4], (DEPTH, 2, D_MODEL, D_FF), f32) * D_MODEL ** -0.5,
        'ffn_w_up': nrm(ks[5], (DEPTH, 2, D_MODEL, D_FF), f32) * D_MODEL ** -0.5,
        'ffn_w_down': nrm(ks[6], (DEPTH, 2, D_FF, D_MODEL), f32) * D_FF ** -0.5,
        's5_lambda_re': -0.5 + 0.01 * nrm(ks[7], (N_S5_LAYERS, 2, S5_GROUPS, S5_STATE), f32),
        's5_lambda_im': lam_im_base + 0.01 * nrm(ks[8], (N_S5_LAYERS, 2, S5_GROUPS, S5_STATE), f32),
        's5_log_step': jax.random.uniform(ks[9], (N_S5_LAYERS, 2, S5_GROUPS), f32,
                                          math.log(S5_DT_MIN), math.log(S5_DT_MAX)),
        's5_b_re': nrm(ks[10], (N_S5_LAYERS, 2, S5_GROUPS, S5_STATE, S5_GROUP), f32) * (2 * S5_GROUP) ** -0.5,
        's5_b_im': nrm(ks[11], (N_S5_LAYERS, 2, S5_GROUPS, S5_STATE, S5_GROUP), f32) * (2 * S5_GROUP) ** -0.5,
        's5_c_re': nrm(ks[12], (N_S5_LAYERS, 2, S5_GROUPS, S5_GROUP, S5_STATE), f32) * 0.5 ** 0.5,
        's5_c_im': nrm(ks[13], (N_S5_LAYERS, 2, S5_GROUPS, S5_GROUP, S5_STATE), f32) * 0.5 ** 0.5,
        's5_d': nrm(ks[14], (N_S5_LAYERS, D_MODEL), f32),
        's5_w_glu_a': nrm(ks[15], (N_S5_LAYERS, D_MODEL, D_MODEL), f32) * D_MODEL ** -0.5,
        's5_w_glu_b': nrm(ks[16], (N_S5_LAYERS, D_MODEL, D_MODEL), f32) * D_MODEL ** -0.5,
        'pool_w': nrm(ks[17], (N_POOL_LAYERS, POOL_GROUPS, POOL_CH, POOL_CH), f32) * POOL_CH ** -0.5,
        'pool_scale': 1.0 + 0.02 * nrm(ks[18], (N_POOL_LAYERS, D_MODEL), f32),
    }


def reference(x_prompt, x_sample, norm_g, final_norm_g, ffn_w_gate, ffn_w_up, ffn_w_down,
              s5_lambda_re, s5_lambda_im, s5_log_step, s5_b_re, s5_b_im, s5_c_re, s5_c_im,
              s5_d, s5_w_glu_a, s5_w_glu_b, pool_w, pool_scale):
    y_prompt = _trunk(x_prompt, norm_g, final_norm_g, ffn_w_gate, ffn_w_up, ffn_w_down,
                      s5_lambda_re, s5_lambda_im, s5_log_step, s5_b_re, s5_b_im, s5_c_re, s5_c_im,
                      s5_d, s5_w_glu_a, s5_w_glu_b, pool_w, pool_scale)
    y_sample = _trunk(x_sample, norm_g, final_norm_g, ffn_w_gate, ffn_w_up, ffn_w_down,
                      s5_lambda_re, s5_lambda_im, s5_log_step, s5_b_re, s5_b_im, s5_c_re, s5_c_im,
                      s5_d, s5_w_glu_a, s5_w_glu_b, pool_w, pool_scale)
    return (y_prompt, y_sample)
```

```python
import functools

import jax
import jax.numpy as jnp
from jax import lax
from jax.experimental import pallas as pl
from jax.experimental.pallas import tpu as pltpu

D_MODEL = 1024
D_FF = 2816
RMS_EPS = 1e-6
S5_GROUP = 16
S5_STATE = 64
POOL_WINDOWS = (2, 4, 8, 16)
POOL_CH = D_MODEL // len(POOL_WINDOWS)
POOL_HALO = 8

LANES = 128
SUBLANES = 8
SLAB = LANES
N_SLAB = D_MODEL // SLAB
GROUPS_PER_SLAB = SLAB // S5_GROUP
HALF = GROUPS_PER_SLAB * S5_STATE
SCOLS = 2 * HALF
CHUNK = SUBLANES
CW = CHUNK * SLAB
ROW_BLOCK = 256
TOKEN_TILE = 512
VMEM_LIMIT = 56 * 1024 * 1024

F32 = jnp.float32
BF16 = jnp.bfloat16


def _params(sem, vmem=VMEM_LIMIT):
    return pltpu.CompilerParams(dimension_semantics=sem, vmem_limit_bytes=vmem)


def _const_spec(shape):
    return pl.BlockSpec(shape, lambda *_: (0,) * len(shape), pipeline_mode=pl.Buffered(1))


def _rms(x, g):
    return x * lax.rsqrt(jnp.mean(x * x, axis=-1, keepdims=True) + RMS_EPS) * g


def _ffn_body(x_ref, g_ref, wg_ref, wu_ref, wd_ref, fg_ref, o_ref, *, final):
    x = x_ref[...]
    h = _rms(x, g_ref[...]).astype(BF16)
    gate = jnp.dot(h, wg_ref[...], preferred_element_type=F32)
    up = jnp.dot(h, wu_ref[...], preferred_element_type=F32)
    act = (gate * jax.nn.sigmoid(gate) * up).astype(BF16)
    y = x + 0.5 * jnp.dot(act, wd_ref[...], preferred_element_type=F32)
    if final:
        y = _rms(y, fg_ref[...])
    o_ref[...] = y


def _ffn(x, g, wg, wu, wd, fg, final):
    n = x.shape[0]
    tile = pl.BlockSpec((TOKEN_TILE, D_MODEL), lambda i: (i, 0))
    return pl.pallas_call(
        functools.partial(_ffn_body, final=final),
        grid=(n // TOKEN_TILE,),
        in_specs=[tile, _const_spec((1, D_MODEL)), _const_spec((D_MODEL, D_FF)), _const_spec((D_MODEL, D_FF)),
                  _const_spec((D_FF, D_MODEL)), _const_spec((1, D_MODEL))],
        out_specs=tile,
        out_shape=jax.ShapeDtypeStruct((n, D_MODEL), F32),
        compiler_params=_params(("parallel",)),
    )(x, g, wg, wu, wd, fg)


def _chunk_rows_body(x_ref, g_ref, o_ref, h_scr):
    h = _rms(x_ref[...], g_ref[...])
    rows = TOKEN_TILE // CHUNK
    for c in range(N_SLAB):
        h_scr[c] = h[:, c * SLAB:(c + 1) * SLAB]
    for t in range(CHUNK):
        for c in range(N_SLAB):
            piece = h_scr[c, pl.ds(t, rows, stride=CHUNK), :]
            o_ref[c, :, t * SLAB:(t + 1) * SLAB] = piece.astype(BF16)


def _chunk_rows(x, g):
    n = x.shape[0]
    rows = TOKEN_TILE // CHUNK
    return pl.pallas_call(
        _chunk_rows_body,
        grid=(n // TOKEN_TILE,),
        in_specs=[pl.BlockSpec((TOKEN_TILE, D_MODEL), lambda i: (i, 0)), _const_spec((1, D_MODEL))],
        out_specs=pl.BlockSpec((N_SLAB, rows, CW), lambda i: (0, i, 0)),
        out_shape=jax.ShapeDtypeStruct((N_SLAB, n // CHUNK, CW), BF16),
        scratch_shapes=[pltpu.VMEM((N_SLAB, TOKEN_TILE, SLAB), F32)],
        compiler_params=_params(("parallel",)),
    )(x, g)


def _cmul(ar, ai, br, bi):
    return ar * br - ai * bi, ar * bi + ai * br


def _s5_prep_body(lr_ref, li_ref, ls_ref, br_ref, bi_ref, cr_ref, ci_ref, d_ref,
                  wst_ref, wout_ref, m_ref, tab_ref):
    lag = []
    row = lax.broadcasted_iota(jnp.int32, (SUBLANES, HALF), 0)
    for d in range(2):
        lr, li = lr_ref[d], li_ref[d]
        step = jnp.exp(ls_ref[d])
        mag = jnp.exp(lr * step)
        ar, ai = mag * jnp.cos(li * step), mag * jnp.sin(li * step)
        den = lr * lr + li * li
        nr = ar - 1.0
        kr, ki = (nr * lr + ai * li) / den, (ai * lr - nr * li) / den
        bbr, bbi = _cmul(kr, ki, br_ref[d], bi_ref[d])
        cre, cim = cr_ref[d], ci_ref[d]
        pw = [(jnp.ones_like(ar), jnp.zeros_like(ar))]
        for _ in range(CHUNK):
            pw.append(_cmul(pw[-1][0], pw[-1][1], ar, ai))
        bs, cs = [], []
        for k in range(CHUNK + 1):
            pr, pi = pw[k]
            xr, xi = _cmul(pr, pi, bbr, bbi)
            bs.append(jnp.concatenate([xr, xi], axis=1))
            yr, yi = _cmul(pr, pi, cre, cim)
            cs.append(jnp.concatenate([yr, -yi], axis=1))
        lag.append([lax.dot_general(bs[k], cs[0], (((1,), (1,)), ((), ())), precision=lax.Precision.HIGHEST,
                                    preferred_element_type=F32) for k in range(CHUNK)])
        for s in range(CHUNK):
            k = CHUNK - 1 - s if d == 0 else s
            wst_ref[s * SLAB:(s + 1) * SLAB, d * SCOLS:(d + 1) * SCOLS] = bs[k].astype(BF16)
        for t in range(CHUNK):
            k = t + 1 if d == 0 else CHUNK - t
            wout_ref[d * SCOLS:(d + 1) * SCOLS, t * SLAB:(t + 1) * SLAB] = cs[k].T.astype(BF16)
        a1 = pw[CHUNK]
        a2 = _cmul(*a1, *a1)
        a4 = _cmul(*a2, *a2)
        for lvl, (p, sh) in enumerate(((a1, 1), (a2, 2), (a4, 4))):
            keep = (row >= sh) if d == 0 else (row <= SUBLANES - 1 - sh)
            tab_ref[d, 2 * lvl] = jnp.where(keep, p[0], 0.0)
            tab_ref[d, 2 * lvl + 1] = jnp.where(keep, p[1], 0.0)
        apr, api = jnp.zeros((SUBLANES, HALF), F32), jnp.zeros((SUBLANES, HALF), F32)
        p = a1
        for n in range(SUBLANES):
            sel = (row == n) if d == 0 else (row == SUBLANES - 1 - n)
            apr, api = jnp.where(sel, p[0], apr), jnp.where(sel, p[1], api)
            p = _cmul(*p, *a1)
        tab_ref[d, 6] = apr
        tab_ref[d, 7] = api
    ri = lax.broadcasted_iota(jnp.int32, (SLAB, SLAB), 0)
    ci = lax.broadcasted_iota(jnp.int32, (SLAB, SLAB), 1)
    diag = lag[0][0] + lag[1][0] + jnp.where(ri == ci, d_ref[...], 0.0)
    for s in range(CHUNK):
        for t in range(CHUNK):
            blk = lag[0][t - s] if t > s else (lag[1][s - t] if s > t else diag)
            m_ref[s * SLAB:(s + 1) * SLAB, t * SLAB:(t + 1) * SLAB] = blk.astype(BF16)


def _s5_prep(lam_re, lam_im, log_step, b_re, b_im, c_re, c_im, d):
    eye = jnp.eye(GROUPS_PER_SLAB, dtype=F32)

    def state_rows(v):
        return v.reshape(2, N_SLAB, 1, HALF).transpose(1, 0, 2, 3)

    def block_diag(v):
        v = v.transpose(1, 0, 2, 3, 4)
        v = v[:, :, :, :, None, :] * eye[None, None, :, None, :, None]
        return v.reshape(N_SLAB, 2, SLAB, HALF)

    ls = jnp.broadcast_to(log_step[:, :, None], lam_re.shape)
    bshape = (2, N_SLAB, GROUPS_PER_SLAB, S5_STATE, S5_GROUP)
    cshape = (2, N_SLAB, GROUPS_PER_SLAB, S5_GROUP, S5_STATE)
    args = (state_rows(lam_re), state_rows(lam_im), state_rows(ls),
            block_diag(b_re.reshape(bshape).transpose(0, 1, 2, 4, 3)), block_diag(b_im.reshape(bshape).transpose(0, 1, 2, 4, 3)),
            block_diag(c_re.reshape(cshape)), block_diag(c_im.reshape(cshape)),
            d.reshape(N_SLAB, 1, SLAB))
    vec = pl.BlockSpec((None, 2, 1, HALF), lambda c: (c, 0, 0, 0))
    mat = pl.BlockSpec((None, 2, SLAB, HALF), lambda c: (c, 0, 0, 0))
    return pl.pallas_call(
        _s5_prep_body,
        grid=(N_SLAB,),
        in_specs=[vec, vec, vec, mat, mat, mat, mat, pl.BlockSpec((None, 1, SLAB), lambda c: (c, 0, 0))],
        out_specs=[pl.BlockSpec((None, CW, 2 * SCOLS), lambda c: (c, 0, 0)),
                   pl.BlockSpec((None, 2 * SCOLS, CW), lambda c: (c, 0, 0)),
                   pl.BlockSpec((None, CW, CW), lambda c: (c, 0, 0)),
                   pl.BlockSpec((None, 2, 8, SUBLANES, HALF), lambda c: (c, 0, 0, 0, 0))],
        out_shape=[jax.ShapeDtypeStruct((N_SLAB, CW, 2 * SCOLS), BF16),
                   jax.ShapeDtypeStruct((N_SLAB, 2 * SCOLS, CW), BF16),
                   jax.ShapeDtypeStruct((N_SLAB, CW, CW), BF16),
                   jax.ShapeDtypeStruct((N_SLAB, 2, 8, SUBLANES, HALF), F32)],
        compiler_params=_params(("parallel",)),
    )(*args)


def _scan_rows(s_ref, tab_ref, d, car_ref, rows):
    shifts = (1, 2, 4) if d == 0 else (SUBLANES - 1, SUBLANES - 2, SUBLANES - 4)
    edge = 0 if d == 0 else SUBLANES - 1
    row = lax.broadcasted_iota(jnp.int32, (SUBLANES, HALF), 0)
    groups = rows // SUBLANES

    def step(n, carry):
        cr, ci = carry
        g = n if d == 0 else groups - 1 - n
        r0 = pl.multiple_of(g * SUBLANES, SUBLANES)
        xr = s_ref[pl.ds(r0, SUBLANES), :HALF]
        xi = s_ref[pl.ds(r0, SUBLANES), HALF:]
        for lvl, sh in enumerate(shifts):
            tr, ti = tab_ref[d, 2 * lvl], tab_ref[d, 2 * lvl + 1]
            sr, si = pltpu.roll(xr, sh, 0), pltpu.roll(xi, sh, 0)
            xr, xi = xr + tr * sr - ti * si, xi + tr * si + ti * sr
        tr, ti = tab_ref[d, 6], tab_ref[d, 7]
        xr, xi = xr + tr * cr - ti * ci, xi + tr * ci + ti * cr
        s_ref[pl.ds(r0, SUBLANES), :HALF] = jnp.where(row == edge, cr, pltpu.roll(xr, shifts[0], 0))
        s_ref[pl.ds(r0, SUBLANES), HALF:] = jnp.where(row == edge, ci, pltpu.roll(xi, shifts[0], 0))
        last = SUBLANES - 1 - edge
        return (jnp.broadcast_to(xr[last:last + 1], xr.shape), jnp.broadcast_to(xi[last:last + 1], xi.shape))

    cr, ci = lax.fori_loop(0, groups, step, (car_ref[d, 0], car_ref[d, 1]))
    car_ref[d, 0] = cr
    car_ref[d, 1] = ci


def _s5_states_body(xf_ref, xb_ref, wst_ref, tab_ref, of_ref, ob_ref, sf_scr, sb_scr, car_scr, *, blocks_per_seq, rows):
    @pl.when(pl.program_id(1) % blocks_per_seq == 0)
    def _():
        car_scr[...] = jnp.zeros_like(car_scr)

    sf_scr[...] = jnp.dot(xf_ref[...], wst_ref[:, :SCOLS], preferred_element_type=F32)
    sb_scr[...] = jnp.dot(xb_ref[...], wst_ref[:, SCOLS:], preferred_element_type=F32)
    _scan_rows(sf_scr, tab_ref, 0, car_scr, rows)
    _scan_rows(sb_scr, tab_ref, 1, car_scr, rows)
    of_ref[...] = sf_scr[...].astype(BF16)
    ob_ref[...] = sb_scr[...].astype(BF16)


def _s5_states(hs, wst, tab, rows_per_seq):
    nrows = hs.shape[1]
    rows = min(ROW_BLOCK, rows_per_seq)
    nb = nrows // rows
    fwd = lambda c, i: (c, i, 0)
    bwd = lambda c, i: (c, nb - 1 - i, 0)
    return pl.pallas_call(
        functools.partial(_s5_states_body, blocks_per_seq=rows_per_seq // rows, rows=rows),
        grid=(N_SLAB, nb),
        in_specs=[pl.BlockSpec((None, rows, CW), fwd), pl.BlockSpec((None, rows, CW), bwd),
                  pl.BlockSpec((None, CW, 2 * SCOLS), lambda c, i: (c, 0, 0)),
                  pl.BlockSpec((None, 2, 8, SUBLANES, HALF), lambda c, i: (c, 0, 0, 0, 0))],
        out_specs=[pl.BlockSpec((None, rows, SCOLS), fwd), pl.BlockSpec((None, rows, SCOLS), bwd)],
        out_shape=[jax.ShapeDtypeStruct((N_SLAB, nrows, SCOLS), BF16)] * 2,
        scratch_shapes=[pltpu.VMEM((rows, SCOLS), F32), pltpu.VMEM((rows, SCOLS), F32),
                        pltpu.VMEM((2, 2, SUBLANES, HALF), F32)],
        compiler_params=_params(("arbitrary", "arbitrary")),
    )(hs, hs, wst, tab)


def _s5_out_body(x_ref, sf_ref, sb_ref, m_ref, wout_ref, o_ref, y_scr, *, rows):
    y = jnp.dot(x_ref[...], m_ref[...], preferred_element_type=F32)
    y += jnp.dot(sf_ref[...], wout_ref[:SCOLS, :], preferred_element_type=F32)
    y += jnp.dot(sb_ref[...], wout_ref[SCOLS:, :], preferred_element_type=F32)
    y_scr[...] = y
    for t in range(CHUNK):
        o_ref[pl.ds(t, rows, stride=CHUNK), :] = y_scr[:, t * SLAB:(t + 1) * SLAB]


def _s5_out(hs, sf, sb, m, wout, rows_per_seq):
    nrows = hs.shape[1]
    rows = min(ROW_BLOCK, rows_per_seq)
    blk = lambda c, i: (c, i, 0)
    return pl.pallas_call(
        functools.partial(_s5_out_body, rows=rows),
        grid=(N_SLAB, nrows // rows),
        in_specs=[pl.BlockSpec((None, rows, CW), blk), pl.BlockSpec((None, rows, SCOLS), blk),
                  pl.BlockSpec((None, rows, SCOLS), blk),
                  pl.BlockSpec((None, CW, CW), lambda c, i: (c, 0, 0)),
                  pl.BlockSpec((None, 2 * SCOLS, CW), lambda c, i: (c, 0, 0))],
        out_specs=pl.BlockSpec((rows * CHUNK, SLAB), lambda c, i: (i, c)),
        out_shape=jax.ShapeDtypeStruct((nrows * CHUNK, D_MODEL), F32),
        scratch_shapes=[pltpu.VMEM((rows, CW), F32)],
        compiler_params=_params(("parallel", "parallel")),
    )(hs, sf, sb, m, wout)


def _s5_glu_body(x_ref, y_ref, wa_ref, wb_ref, o_ref):
    g = jax.nn.gelu(y_ref[...]).astype(BF16)
    a = jnp.dot(g, wa_ref[...], preferred_element_type=F32)
    b = jnp.dot(g, wb_ref[...], preferred_element_type=F32)
    o_ref[...] = x_ref[...] + a * jax.nn.sigmoid(b)


def _s5_glu(x, y, wa, wb):
    n = x.shape[0]
    tile = pl.BlockSpec((TOKEN_TILE, D_MODEL), lambda i: (i, 0))
    return pl.pallas_call(
        _s5_glu_body,
        grid=(n // TOKEN_TILE,),
        in_specs=[tile, tile, _const_spec((D_MODEL, D_MODEL)), _const_spec((D_MODEL, D_MODEL))],
        out_specs=tile,
        out_shape=jax.ShapeDtypeStruct((n, D_MODEL), F32),
        compiler_params=_params(("parallel",)),
    )(x, y, wa, wb)


def _pool_body(x_ref, prev_ref, next_ref, g_ref, w_ref, sc_ref, o_ref, e_scr, s_scr, *, tiles_per_seq, seq_len):
    j = pl.program_id(0) % tiles_per_seq
    g = g_ref[...]
    x = x_ref[...]
    h = _rms(x, g)
    lo, hi = 2 * POOL_HALO, 2 * POOL_HALO + TOKEN_TILE
    e_scr[0:POOL_HALO, :] = jnp.zeros((POOL_HALO, D_MODEL), F32)
    e_scr[hi + POOL_HALO:hi + 2 * POOL_HALO, :] = jnp.zeros((POOL_HALO, D_MODEL), F32)
    e_scr[lo - POOL_HALO:lo, :] = jnp.where(j == 0, 0.0, _rms(prev_ref[...], g))
    e_scr[hi:hi + POOL_HALO, :] = jnp.where(j == tiles_per_seq - 1, 0.0, _rms(next_ref[...], g))
    e_scr[lo:hi, :] = h
    pos = j * TOKEN_TILE + lax.broadcasted_iota(jnp.int32, (TOKEN_TILE, 1), 0)
    for gi, win in enumerate(POOL_WINDOWS):
        cols = slice(gi * POOL_CH, (gi + 1) * POOL_CH)
        ext = POOL_HALO - 1
        s_scr[lo - ext:hi + ext, :] = e_scr[lo - ext - 1:hi + ext - 1, cols] + e_scr[lo - ext:hi + ext, cols]
        w = 2
        while w < win:
            half = w // 2
            ext -= half
            s_new = s_scr[lo - ext - half:hi + ext - half, :] + s_scr[lo - ext + half:hi + ext + half, :]
            s_scr[lo - ext:hi + ext, :] = s_new
            w *= 2
        cnt = jnp.minimum(pos + (win - win // 2), seq_len) - jnp.maximum(pos - win // 2, 0)
        p = s_scr[lo:hi, :] / cnt.astype(F32) - h[:, cols]
        z = jnp.dot(p.astype(BF16), w_ref[gi], preferred_element_type=F32)
        o_ref[:, cols] = x[:, cols] + z * sc_ref[:, cols]


def _pool(x, g, w, scale, seq_len):
    n = x.shape[0]
    halo_blocks = TOKEN_TILE // POOL_HALO
    last = n // POOL_HALO - 1
    tile = pl.BlockSpec((TOKEN_TILE, D_MODEL), lambda i: (i, 0))
    return pl.pallas_call(
        functools.partial(_pool_body, tiles_per_seq=seq_len // TOKEN_TILE, seq_len=seq_len),
        grid=(n // TOKEN_TILE,),
        in_specs=[tile,
                  pl.BlockSpec((POOL_HALO, D_MODEL), lambda i: (jnp.maximum(i * halo_blocks - 1, 0), 0)),
                  pl.BlockSpec((POOL_HALO, D_MODEL), lambda i: (jnp.minimum((i + 1) * halo_blocks, last), 0)),
                  _const_spec((1, D_MODEL)), _const_spec((len(POOL_WINDOWS), POOL_CH, POOL_CH)),
                  _const_spec((1, D_MODEL))],
        out_specs=tile,
        out_shape=jax.ShapeDtypeStruct((n, D_MODEL), F32),
        scratch_shapes=[pltpu.VMEM((TOKEN_TILE + 4 * POOL_HALO, D_MODEL), F32),
                        pltpu.VMEM((TOKEN_TILE + 4 * POOL_HALO, POOL_CH), F32)],
        compiler_params=_params(("parallel",)),
    )(x, x, x, g, w, scale)


def _trunk(x, seq_len, norm_g, final_norm_g, wg, wu, wd, s5, s5_wa, s5_wb, pool_w, pool_scale):
    depth = norm_g.shape[0]
    row = lambda v: v.reshape(1, D_MODEL)
    for layer in range(depth):
        g = norm_g[layer]
        x = _ffn(x, row(g[0]), wg[layer, 0], wu[layer, 0], wd[layer, 0], row(final_norm_g), False)
        j = layer // 2
        if layer % 2 == 0:
            wst, wout, m, tab = s5[j]
            hs = _chunk_rows(x, row(g[1]))
            sf, sb = _s5_states(hs, wst, tab, seq_len // CHUNK)
            y = _s5_out(hs, sf, sb, m, wout, seq_len // CHUNK)
            x = _s5_glu(x, y, s5_wa[j], s5_wb[j])
        else:
            x = _pool(x, row(g[1]), pool_w[j], row(pool_scale[j]), seq_len)
        x = _ffn(x, row(g[2]), wg[layer, 1], wu[layer, 1], wd[layer, 1], row(final_norm_g), layer == depth - 1)
    return x


def kernel(x_prompt, x_sample, norm_g, final_norm_g, ffn_w_gate, ffn_w_up, ffn_w_down, s5_lambda_re, s5_lambda_im, s5_log_step, s5_b_re, s5_b_im, s5_c_re, s5_c_im, s5_d, s5_w_glu_a, s5_w_glu_b, pool_w, pool_scale):
    wg, wu, wd = ffn_w_gate.astype(BF16), ffn_w_up.astype(BF16), ffn_w_down.astype(BF16)
    wa, wb, pw = s5_w_glu_a.astype(BF16), s5_w_glu_b.astype(BF16), pool_w.astype(BF16)
    s5 = [_s5_prep(s5_lambda_re[j], s5_lambda_im[j], s5_log_step[j], s5_b_re[j], s5_b_im[j], s5_c_re[j], s5_c_im[j],
                   s5_d[j]) for j in range(s5_lambda_re.shape[0])]
    outs = []
    for x in (x_prompt, x_sample):
        bsz, seq, _ = x.shape
        y = _trunk(x.reshape(bsz * seq, D_MODEL), seq, norm_g, final_norm_g, wg, wu, wd, s5, wa, wb, pw, pool_scale)
        outs.append(y.reshape(bsz, seq, D_MODEL))
    return tuple(outs)
```

```python
import functools

import jax
import jax.numpy as jnp
from jax import lax
from jax.experimental import pallas as pl
from jax.experimental.pallas import tpu as pltpu

D_MODEL = 1024
D_FF = 2816
RMS_EPS = 1e-6
S5_GROUP = 16
S5_STATE = 64
POOL_WINDOWS = (2, 4, 8, 16)
POOL_CH = D_MODEL // len(POOL_WINDOWS)
POOL_HALO = 8

LANES = 128
SUBLANES = 8
SLAB = LANES
N_SLAB = D_MODEL // SLAB
GROUPS_PER_SLAB = SLAB // S5_GROUP
HALF = GROUPS_PER_SLAB * S5_STATE
SCOLS = 2 * HALF
CHUNK = SUBLANES
CW = CHUNK * SLAB
ROW_BLOCK = 256
TOKEN_TILE = 512
VMEM_LIMIT = 56 * 1024 * 1024

F32 = jnp.float32
BF16 = jnp.bfloat16


def _params(sem, vmem=VMEM_LIMIT):
    return pltpu.CompilerParams(dimension_semantics=sem, vmem_limit_bytes=vmem)


def _pick(tail, *lead):
    tail, lead = tuple(tail), tuple(lead)
    return pl.BlockSpec((None,) * len(lead) + tail, lambda *_: lead + (0,) * len(tail), pipeline_mode=pl.Buffered(1))


def _rms(x, g):
    return x * lax.rsqrt(jnp.mean(x * x, axis=-1, keepdims=True) + RMS_EPS) * g


def _pool_mix(x, prev_ref, next_ref, g, w_ref, scale, e_scr, s_scr, j, tiles_per_seq, seq_len):
    h = _rms(x, g)
    lo, hi = 2 * POOL_HALO, 2 * POOL_HALO + TOKEN_TILE
    e_scr[lo - POOL_HALO:lo, :] = jnp.where(j == 0, 0.0, _rms(prev_ref[...], g))
    e_scr[hi:hi + POOL_HALO, :] = jnp.where(j == tiles_per_seq - 1, 0.0, _rms(next_ref[...], g))
    e_scr[lo:hi, :] = h
    pos = j * TOKEN_TILE + lax.broadcasted_iota(jnp.int32, (TOKEN_TILE, 1), 0)
    parts = []
    for gi, win in enumerate(POOL_WINDOWS):
        cols = slice(gi * POOL_CH, (gi + 1) * POOL_CH)
        ext = POOL_HALO - 1
        s_scr[lo - ext:hi + ext, :] = e_scr[lo - ext - 1:hi + ext - 1, cols] + e_scr[lo - ext:hi + ext, cols]
        w = 2
        while w < win:
            half = w // 2
            ext -= half
            s_new = s_scr[lo - ext - half:hi + ext - half, :] + s_scr[lo - ext + half:hi + ext + half, :]
            s_scr[lo - ext:hi + ext, :] = s_new
            w *= 2
        cnt = jnp.minimum(pos + (win - win // 2), seq_len) - jnp.maximum(pos - win // 2, 0)
        p = s_scr[lo:hi, :] / cnt.astype(F32) - h[:, cols]
        parts.append(jnp.dot(p.astype(BF16), w_ref[gi], preferred_element_type=F32))
    return jnp.concatenate(parts, axis=1) * scale


def _ffn_body(*refs, pre, post, tiles_per_seq, seq_len):
    it = iter(refs)
    x_ref = next(it)
    if pre == "glu":
        y_ref, wa_ref, wb_ref = next(it), next(it), next(it)
    if pre == "pool":
        prev_ref, next_ref, pg_ref, pw_ref, ps_ref = next(it), next(it), next(it), next(it), next(it)
    g_ref, wg_ref, wu_ref, wd_ref = next(it), next(it), next(it), next(it)
    if post is not None:
        g2_ref = next(it)
    o_ref = next(it)
    if post == "rows":
        hs_ref = next(it)
    if pre == "pool":
        e_scr, s_scr = next(it), next(it)
    if post == "rows":
        h_scr = next(it)

    x = x_ref[...]
    if pre == "glu":
        gy = jax.nn.gelu(y_ref[...]).astype(BF16)
        a = jnp.dot(gy, wa_ref[...], preferred_element_type=F32)
        b = jnp.dot(gy, wb_ref[...], preferred_element_type=F32)
        x = x + a * jax.nn.sigmoid(b)
    if pre == "pool":
        j = pl.program_id(0) % tiles_per_seq
        x = x + _pool_mix(x, prev_ref, next_ref, pg_ref[...], pw_ref, ps_ref[...], e_scr, s_scr, j, tiles_per_seq, seq_len)
    h = _rms(x, g_ref[...]).astype(BF16)
    gate = jnp.dot(h, wg_ref[...], preferred_element_type=F32)
    up = jnp.dot(h, wu_ref[...], preferred_element_type=F32)
    act = (gate * jax.nn.sigmoid(gate) * up).astype(BF16)
    y = x + 0.5 * jnp.dot(act, wd_ref[...], preferred_element_type=F32)
    if post == "final":
        y = _rms(y, g2_ref[...])
    o_ref[...] = y
    if post == "rows":
        h2 = _rms(y, g2_ref[...])
        rows = TOKEN_TILE // CHUNK
        for c in range(N_SLAB):
            h_scr[c] = h2[:, c * SLAB:(c + 1) * SLAB]
        for t in range(CHUNK):
            for c in range(N_SLAB):
                hs_ref[c, :, t * SLAB:(t + 1) * SLAB] = h_scr[c, pl.ds(t, rows, stride=CHUNK), :].astype(BF16)


def _ffn(x, g_spec, w_idx, norm_g, wg, wu, wd, *, seq_len, pre=None, pre_args=(), post=None, g2=None, g2_spec=None):
    n = x.shape[0]
    tile = pl.BlockSpec((TOKEN_TILE, D_MODEL), lambda i: (i, 0))
    args, specs, scratch = [x], [tile], []
    if pre == "glu":
        y, wa, wb, j = pre_args
        args += [y, wa, wb]
        specs += [tile, _pick((D_MODEL, D_MODEL), j), _pick((D_MODEL, D_MODEL), j)]
    if pre == "pool":
        pg_spec, pw, ps, j = pre_args
        halo_blocks = TOKEN_TILE // POOL_HALO
        last = n // POOL_HALO - 1
        args += [x, x, norm_g, pw, ps]
        specs += [pl.BlockSpec((POOL_HALO, D_MODEL), lambda i: (jnp.maximum(i * halo_blocks - 1, 0), 0)),
                  pl.BlockSpec((POOL_HALO, D_MODEL), lambda i: (jnp.minimum((i + 1) * halo_blocks, last), 0)),
                  pg_spec, _pick((len(POOL_WINDOWS), POOL_CH, POOL_CH), j), _pick((1, D_MODEL), j)]
        scratch += [pltpu.VMEM((TOKEN_TILE + 4 * POOL_HALO, D_MODEL), F32),
                    pltpu.VMEM((TOKEN_TILE + 4 * POOL_HALO, POOL_CH), F32)]
    args += [norm_g, wg, wu, wd]
    specs += [g_spec, _pick((D_MODEL, D_FF), *w_idx), _pick((D_MODEL, D_FF), *w_idx), _pick((D_FF, D_MODEL), *w_idx)]
    if post is not None:
        args.append(g2)
        specs.append(g2_spec)
    out_shape, out_specs = [jax.ShapeDtypeStruct((n, D_MODEL), F32)], [tile]
    if post == "rows":
        rows = TOKEN_TILE // CHUNK
        out_shape.append(jax.ShapeDtypeStruct((N_SLAB, n // CHUNK, CW), BF16))
        out_specs.append(pl.BlockSpec((N_SLAB, rows, CW), lambda i: (0, i, 0)))
        scratch.append(pltpu.VMEM((N_SLAB, TOKEN_TILE, SLAB), F32))
    return pl.pallas_call(
        functools.partial(_ffn_body, pre=pre, post=post, tiles_per_seq=seq_len // TOKEN_TILE, seq_len=seq_len),
        grid=(n // TOKEN_TILE,),
        in_specs=specs, out_specs=out_specs, out_shape=out_shape, scratch_shapes=scratch,
        compiler_params=_params(("parallel",)),
    )(*args)


def _cmul(ar, ai, br, bi):
    return ar * br - ai * bi, ar * bi + ai * br


def _s5_prep_body(lr_ref, li_ref, ls_ref, br_ref, bi_ref, cr_ref, ci_ref, d_ref,
                  wst_ref, wout_ref, m_ref, a_ref):
    lag = []
    for d in range(2):
        lr, li = lr_ref[d], li_ref[d]
        step = jnp.exp(ls_ref[d])
        mag = jnp.exp(lr * step)
        ar, ai = mag * jnp.cos(li * step), mag * jnp.sin(li * step)
        den = lr * lr + li * li
        nr = ar - 1.0
        kr, ki = (nr * lr + ai * li) / den, (ai * lr - nr * li) / den
        bbr, bbi = _cmul(kr, ki, br_ref[d], bi_ref[d])
        cre, cim = cr_ref[d], ci_ref[d]
        pw = [(jnp.ones_like(ar), jnp.zeros_like(ar))]
        for _ in range(CHUNK):
            pw.append(_cmul(pw[-1][0], pw[-1][1], ar, ai))
        bs, cs = [], []
        for k in range(CHUNK + 1):
            pr, pi = pw[k]
            xr, xi = _cmul(pr, pi, bbr, bbi)
            bs.append(jnp.concatenate([xr, xi], axis=1))
            yr, yi = _cmul(pr, pi, cre, cim)
            cs.append(jnp.concatenate([yr, -yi], axis=1))
        lag.append([lax.dot_general(bs[k], cs[0], (((1,), (1,)), ((), ())), precision=lax.Precision.HIGHEST,
                                    preferred_element_type=F32) for k in range(CHUNK)])
        for s in range(CHUNK):
            k = CHUNK - 1 - s if d == 0 else s
            wst_ref[d, s * SLAB:(s + 1) * SLAB, :] = bs[k].astype(BF16)
        for t in range(CHUNK):
            k = t + 1 if d == 0 else CHUNK - t
            wout_ref[d * SCOLS:(d + 1) * SCOLS, t * SLAB:(t + 1) * SLAB] = cs[k].T.astype(BF16)
        a_ref[d, 0] = pw[CHUNK][0]
        a_ref[d, 1] = pw[CHUNK][1]
    ri = lax.broadcasted_iota(jnp.int32, (SLAB, SLAB), 0)
    ci = lax.broadcasted_iota(jnp.int32, (SLAB, SLAB), 1)
    diag = lag[0][0] + lag[1][0] + jnp.where(ri == ci, d_ref[...], 0.0)
    for s in range(CHUNK):
        for t in range(CHUNK):
            blk = lag[0][t - s] if t > s else (lag[1][s - t] if s > t else diag)
            m_ref[s * SLAB:(s + 1) * SLAB, t * SLAB:(t + 1) * SLAB] = blk.astype(BF16)


def _s5_prep(lam_re, lam_im, log_step, b_re, b_im, c_re, c_im, d):
    eye = jnp.eye(GROUPS_PER_SLAB, dtype=F32)

    def state_rows(v):
        return v.reshape(2, N_SLAB, 1, HALF).transpose(1, 0, 2, 3)

    def block_diag(v):
        v = v.transpose(1, 0, 2, 3, 4)
        v = v[:, :, :, :, None, :] * eye[None, None, :, None, :, None]
        return v.reshape(N_SLAB, 2, SLAB, HALF)

    ls = jnp.broadcast_to(log_step[:, :, None], lam_re.shape)
    bshape = (2, N_SLAB, GROUPS_PER_SLAB, S5_STATE, S5_GROUP)
    cshape = (2, N_SLAB, GROUPS_PER_SLAB, S5_GROUP, S5_STATE)
    args = (state_rows(lam_re), state_rows(lam_im), state_rows(ls),
            block_diag(b_re.reshape(bshape).transpose(0, 1, 2, 4, 3)), block_diag(b_im.reshape(bshape).transpose(0, 1, 2, 4, 3)),
            block_diag(c_re.reshape(cshape)), block_diag(c_im.reshape(cshape)),
            d.reshape(N_SLAB, 1, SLAB))
    vec = pl.BlockSpec((None, 2, 1, HALF), lambda c: (c, 0, 0, 0))
    mat = pl.BlockSpec((None, 2, SLAB, HALF), lambda c: (c, 0, 0, 0))
    wst, wout, m, a = pl.pallas_call(
        _s5_prep_body,
        grid=(N_SLAB,),
        in_specs=[vec, vec, vec, mat, mat, mat, mat, pl.BlockSpec((None, 1, SLAB), lambda c: (c, 0, 0))],
        out_specs=[pl.BlockSpec((2, None, CW, SCOLS), lambda c: (0, c, 0, 0)),
                   pl.BlockSpec((None, 2 * SCOLS, CW), lambda c: (c, 0, 0)),
                   pl.BlockSpec((None, CW, CW), lambda c: (c, 0, 0)),
                   pl.BlockSpec((None, 2, 2, 1, HALF), lambda c: (c, 0, 0, 0, 0))],
        out_shape=[jax.ShapeDtypeStruct((2, N_SLAB, CW, SCOLS), BF16),
                   jax.ShapeDtypeStruct((N_SLAB, 2 * SCOLS, CW), BF16),
                   jax.ShapeDtypeStruct((N_SLAB, CW, CW), BF16),
                   jax.ShapeDtypeStruct((N_SLAB, 2, 2, 1, HALF), F32)],
        compiler_params=_params(("parallel",)),
    )(*args)
    return wst, wout, m, a.reshape(N_SLAB, 2, 2, HALF).transpose(1, 2, 0, 3)


def _s5_states_body(x_ref, wst_ref, a_ref, o_ref, s_scr, car_scr, *, blocks_per_seq, rows):
    d = pl.program_id(0)
    pieces = SCOLS // LANES
    half = pieces // 2

    @pl.when(pl.program_id(1) % blocks_per_seq == 0)
    def _():
        car_scr[...] = jnp.zeros_like(car_scr)

    for c in range(N_SLAB):
        s_loc = jnp.dot(x_ref[c], wst_ref[c], preferred_element_type=F32)
        for q in range(pieces):
            s_scr[q, pl.ds(c, rows, stride=N_SLAB), :] = s_loc[:, q * LANES:(q + 1) * LANES]

    ar = [a_ref[0, :, q * LANES:(q + 1) * LANES] for q in range(half)]
    ai = [a_ref[1, :, q * LANES:(q + 1) * LANES] for q in range(half)]

    def step(n, carry):
        r = n + d * (rows - 1 - 2 * n)
        r0 = pl.multiple_of(r * N_SLAB, N_SLAB)
        new = []
        for q in range(half):
            sr, si = carry[q], carry[half + q]
            xr, xi = s_scr[q, pl.ds(r0, N_SLAB), :], s_scr[half + q, pl.ds(r0, N_SLAB), :]
            s_scr[q, pl.ds(r0, N_SLAB), :] = sr
            s_scr[half + q, pl.ds(r0, N_SLAB), :] = si
            new.append((ar[q] * sr - ai[q] * si + xr, ar[q] * si + ai[q] * sr + xi))
        return tuple(v[0] for v in new) + tuple(v[1] for v in new)

    carry = lax.fori_loop(0, rows, step, tuple(car_scr[q] for q in range(pieces)), unroll=4)
    for q in range(pieces):
        car_scr[q] = carry[q]
    for c in range(N_SLAB):
        for q in range(pieces):
            o_ref[c, :, q * LANES:(q + 1) * LANES] = s_scr[q, pl.ds(c, rows, stride=N_SLAB), :].astype(BF16)


def _s5_states(hs, wst, a, rows_per_seq):
    nrows = hs.shape[1]
    rows = min(ROW_BLOCK, rows_per_seq)
    nb = nrows // rows
    blk = lambda d, i: i + d * (nb - 1 - 2 * i)
    return pl.pallas_call(
        functools.partial(_s5_states_body, blocks_per_seq=rows_per_seq // rows, rows=rows),
        grid=(2, nb),
        in_specs=[pl.BlockSpec((N_SLAB, rows, CW), lambda d, i: (0, blk(d, i), 0)),
                  pl.BlockSpec((None, N_SLAB, CW, SCOLS), lambda d, i: (d, 0, 0, 0), pipeline_mode=pl.Buffered(1)),
                  pl.BlockSpec((None, 2, N_SLAB, HALF), lambda d, i: (d, 0, 0, 0))],
        out_specs=pl.BlockSpec((None, N_SLAB, rows, SCOLS), lambda d, i: (d, 0, blk(d, i), 0)),
        out_shape=jax.ShapeDtypeStruct((2, N_SLAB, nrows, SCOLS), BF16),
        scratch_shapes=[pltpu.VMEM((SCOLS // LANES, rows * N_SLAB, LANES), F32),
                        pltpu.VMEM((SCOLS // LANES, N_SLAB, LANES), F32)],
        compiler_params=_params(("arbitrary", "arbitrary")),
    )(hs, wst, a)


def _s5_out_body(x_ref, sf_ref, sb_ref, m_ref, wout_ref, o_ref, y_scr, *, rows):
    y = jnp.dot(x_ref[...], m_ref[...], preferred_element_type=F32)
    y += jnp.dot(sf_ref[...], wout_ref[:SCOLS, :], preferred_element_type=F32)
    y += jnp.dot(sb_ref[...], wout_ref[SCOLS:, :], preferred_element_type=F32)
    y_scr[...] = y
    for t in range(CHUNK):
        o_ref[pl.ds(t, rows, stride=CHUNK), :] = y_scr[:, t * SLAB:(t + 1) * SLAB]


def _s5_out(hs, s, m, wout, rows_per_seq):
    nrows = hs.shape[1]
    rows = min(ROW_BLOCK, rows_per_seq)
    return pl.pallas_call(
        functools.partial(_s5_out_body, rows=rows),
        grid=(N_SLAB, nrows // rows),
        in_specs=[pl.BlockSpec((None, rows, CW), lambda c, i: (c, i, 0)),
                  pl.BlockSpec((None, None, rows, SCOLS), lambda c, i: (0, c, i, 0)),
                  pl.BlockSpec((None, None, rows, SCOLS), lambda c, i: (1, c, i, 0)),
                  pl.BlockSpec((None, CW, CW), lambda c, i: (c, 0, 0)),
                  pl.BlockSpec((None, 2 * SCOLS, CW), lambda c, i: (c, 0, 0))],
        out_specs=pl.BlockSpec((rows * CHUNK, SLAB), lambda c, i: (i, c)),
        out_shape=jax.ShapeDtypeStruct((nrows * CHUNK, D_MODEL), F32),
        scratch_shapes=[pltpu.VMEM((rows, CW), F32)],
        compiler_params=_params(("parallel", "parallel")),
    )(hs, s, s, m, wout)


def _trunk(x, seq_len, norm_g, final_norm_g, wg, wu, wd, s5, s5_wa, s5_wb, pool_w, pool_scale):
    depth = norm_g.shape[0]
    gain = lambda layer, k: _pick((1, D_MODEL), layer, k)
    ffn = functools.partial(_ffn, norm_g=norm_g, wg=wg, wu=wu, wd=wd, seq_len=seq_len)
    for layer in range(depth):
        j = layer // 2
        last = dict(post="final", g2=final_norm_g, g2_spec=_pick((1, D_MODEL))) if layer == depth - 1 else {}
        if layer % 2 == 0:
            wst, wout, m, a = s5[j]
            x, hs = ffn(x, gain(layer, 0), (layer, 0), post="rows", g2=norm_g, g2_spec=gain(layer, 1))
            s = _s5_states(hs, wst, a, seq_len // CHUNK)
            y = _s5_out(hs, s, m, wout, seq_len // CHUNK)
            x = ffn(x, gain(layer, 2), (layer, 1), pre="glu", pre_args=(y, s5_wa, s5_wb, j), **last)[0]
        else:
            x = ffn(x, gain(layer, 0), (layer, 0))[0]
            x = ffn(x, gain(layer, 2), (layer, 1), pre="pool", pre_args=(gain(layer, 1), pool_w, pool_scale, j), **last)[0]
    return x


def kernel(x_prompt, x_sample, norm_g, final_norm_g, ffn_w_gate, ffn_w_up, ffn_w_down, s5_lambda_re, s5_lambda_im, s5_log_step, s5_b_re, s5_b_im, s5_c_re, s5_c_im, s5_d, s5_w_glu_a, s5_w_glu_b, pool_w, pool_scale):
    wg, wu, wd = ffn_w_gate.astype(BF16), ffn_w_up.astype(BF16), ffn_w_down.astype(BF16)
    wa, wb, pw = s5_w_glu_a.astype(BF16), s5_w_glu_b.astype(BF16), pool_w.astype(BF16)
    norm_g = norm_g.reshape(norm_g.shape[0], 3, 1, D_MODEL)
    final_norm_g = final_norm_g.reshape(1, D_MODEL)
    pool_scale = pool_scale.reshape(-1, 1, D_MODEL)
    s5 = [_s5_prep(s5_lambda_re[j], s5_lambda_im[j], s5_log_step[j], s5_b_re[j], s5_b_im[j], s5_c_re[j], s5_c_im[j],
                   s5_d[j]) for j in range(s5_lambda_re.shape[0])]
    outs = []
    for x in (x_prompt, x_sample):
        bsz, seq, _ = x.shape
        y = _trunk(x.reshape(bsz * seq, D_MODEL), seq, norm_g, final_norm_g, wg, wu, wd, s5, wa, wb, pw, pool_scale)
        outs.append(y.reshape(bsz, seq, D_MODEL))
    return tuple(outs)
```

```python
import functools

import jax
import jax.numpy as jnp
from jax import lax
from jax.experimental import pallas as pl
from jax.experimental.pallas import tpu as pltpu

D_MODEL = 1024
D_FF = 2816
RMS_EPS = 1e-6
S5_GROUP = 16
S5_STATE = 64
POOL_WINDOWS = (2, 4, 8, 16)
POOL_CH = D_MODEL // len(POOL_WINDOWS)
POOL_HALO = 8

LANES = 128
SUBLANES = 8
SLAB = LANES
N_SLAB = D_MODEL // SLAB
PART = 32
PARTS_PER_SLAB = SLAB // PART
N_PART = D_MODEL // PART
PSTATE = (PART // S5_GROUP) * S5_STATE
PCOLS = 2 * PSTATE
CHUNK = SUBLANES
PW = CHUNK * PART
N_SET = N_PART // SUBLANES
ROW_BLOCK = 128
TOKEN_TILE = 512
VMEM_LIMIT = 56 * 1024 * 1024

F32 = jnp.float32
BF16 = jnp.bfloat16


def _params(sem, vmem=VMEM_LIMIT):
    return pltpu.CompilerParams(dimension_semantics=sem, vmem_limit_bytes=vmem)


def _pick(tail, *lead):
    tail, lead = tuple(tail), tuple(lead)
    return pl.BlockSpec((None,) * len(lead) + tail, lambda *_: lead + (0,) * len(tail), pipeline_mode=pl.Buffered(1))


def _rms(x, g):
    return x * lax.rsqrt(jnp.mean(x * x, axis=-1, keepdims=True) + RMS_EPS) * g


def _swap_blocks(a):
    lane = lax.broadcasted_iota(jnp.int32, a[0].shape, 1)
    low = lane < 2 * PART
    even = (lane & PART) == 0
    b0 = jnp.where(low, a[0], pltpu.roll(a[2], 2 * PART, 1))
    b2 = jnp.where(low, pltpu.roll(a[0], 2 * PART, 1), a[2])
    b1 = jnp.where(low, a[1], pltpu.roll(a[3], 2 * PART, 1))
    b3 = jnp.where(low, pltpu.roll(a[1], 2 * PART, 1), a[3])
    return [jnp.where(even, b0, pltpu.roll(b1, PART, 1)), jnp.where(even, pltpu.roll(b0, 3 * PART, 1), b1),
            jnp.where(even, b2, pltpu.roll(b3, PART, 1)), jnp.where(even, pltpu.roll(b2, 3 * PART, 1), b3)]


def _pool_mix(x, prev_ref, next_ref, g, w_ref, scale, e_scr, s_scr, j, tiles_per_seq, seq_len):
    h = _rms(x, g)
    lo, hi = 2 * POOL_HALO, 2 * POOL_HALO + TOKEN_TILE
    e_scr[lo - POOL_HALO:lo, :] = jnp.where(j == 0, 0.0, _rms(prev_ref[...], g))
    e_scr[hi:hi + POOL_HALO, :] = jnp.where(j == tiles_per_seq - 1, 0.0, _rms(next_ref[...], g))
    e_scr[lo:hi, :] = h
    pos = j * TOKEN_TILE + lax.broadcasted_iota(jnp.int32, (TOKEN_TILE, 1), 0)
    parts = []
    for gi, win in enumerate(POOL_WINDOWS):
        cols = slice(gi * POOL_CH, (gi + 1) * POOL_CH)
        ext = POOL_HALO - 1
        s_scr[lo - ext:hi + ext, :] = e_scr[lo - ext - 1:hi + ext - 1, cols] + e_scr[lo - ext:hi + ext, cols]
        w = 2
        while w < win:
            half = w // 2
            ext -= half
            s_new = s_scr[lo - ext - half:hi + ext - half, :] + s_scr[lo - ext + half:hi + ext + half, :]
            s_scr[lo - ext:hi + ext, :] = s_new
            w *= 2
        cnt = jnp.minimum(pos + (win - win // 2), seq_len) - jnp.maximum(pos - win // 2, 0)
        p = s_scr[lo:hi, :] / cnt.astype(F32) - h[:, cols]
        parts.append(jnp.dot(p.astype(BF16), w_ref[gi], preferred_element_type=F32))
    return jnp.concatenate(parts, axis=1) * scale


def _ffn_body(*refs, pre, post, tiles_per_seq, seq_len):
    it = iter(refs)
    x_ref = next(it)
    if pre == "glu":
        y_ref, wa_ref, wb_ref = next(it), next(it), next(it)
    if pre == "pool":
        prev_ref, next_ref, pg_ref, pw_ref, ps_ref = next(it), next(it), next(it), next(it), next(it)
    g_ref, wg_ref, wu_ref, wd_ref = next(it), next(it), next(it), next(it)
    if post is not None:
        g2_ref = next(it)
    o_ref = next(it)
    if post == "rows":
        hs_ref = next(it)
    if pre == "pool":
        e_scr, s_scr = next(it), next(it)
    if post == "rows":
        h_scr = next(it)

    x = x_ref[...]
    if pre == "glu":
        y = jnp.concatenate([y_ref[c] for c in range(N_SLAB)], axis=1)
        gy = jax.nn.gelu(y).astype(BF16)
        a = jnp.dot(gy, wa_ref[...], preferred_element_type=F32)
        b = jnp.dot(gy, wb_ref[...], preferred_element_type=F32)
        x = x + a * jax.nn.sigmoid(b)
    if pre == "pool":
        j = pl.program_id(0) % tiles_per_seq
        x = x + _pool_mix(x, prev_ref, next_ref, pg_ref[...], pw_ref, ps_ref[...], e_scr, s_scr, j, tiles_per_seq, seq_len)
    h = _rms(x, g_ref[...]).astype(BF16)
    gate = jnp.dot(h, wg_ref[...], preferred_element_type=F32)
    up = jnp.dot(h, wu_ref[...], preferred_element_type=F32)
    act = (gate * jax.nn.sigmoid(gate) * up).astype(BF16)
    y = x + 0.5 * jnp.dot(act, wd_ref[...], preferred_element_type=F32)
    if post == "final":
        y = _rms(y, g2_ref[...])
    o_ref[...] = y
    if post == "rows":
        h2 = _rms(y, g2_ref[...])
        rows = TOKEN_TILE // CHUNK
        for c in range(N_SLAB):
            h_scr[c] = h2[:, c * SLAB:(c + 1) * SLAB]
        for c in range(N_SLAB):
            for u in range(CHUNK // PARTS_PER_SLAB):
                tok = [h_scr[c, pl.ds(PARTS_PER_SLAB * u + m, rows, stride=CHUNK), :] for m in range(PARTS_PER_SLAB)]
                for k, piece in enumerate(_swap_blocks(tok)):
                    hs_ref[PARTS_PER_SLAB * c + k, :, u * LANES:(u + 1) * LANES] = piece.astype(BF16)


def _ffn(x, g_spec, w_idx, norm_g, wg, wu, wd, *, seq_len, pre=None, pre_args=(), post=None, g2=None, g2_spec=None):
    n = x.shape[0]
    tile = pl.BlockSpec((TOKEN_TILE, D_MODEL), lambda i: (i, 0))
    args, specs, scratch = [x], [tile], []
    if pre == "glu":
        y, wa, wb, j = pre_args
        args += [y, wa, wb]
        specs += [pl.BlockSpec((N_SLAB, TOKEN_TILE, SLAB), lambda i: (0, i, 0)),
                  _pick((D_MODEL, D_MODEL), j), _pick((D_MODEL, D_MODEL), j)]
    if pre == "pool":
        pg_spec, pw, ps, j = pre_args
        halo_blocks = TOKEN_TILE // POOL_HALO
        last = n // POOL_HALO - 1
        args += [x, x, norm_g, pw, ps]
        specs += [pl.BlockSpec((POOL_HALO, D_MODEL), lambda i: (jnp.maximum(i * halo_blocks - 1, 0), 0)),
                  pl.BlockSpec((POOL_HALO, D_MODEL), lambda i: (jnp.minimum((i + 1) * halo_blocks, last), 0)),
                  pg_spec, _pick((len(POOL_WINDOWS), POOL_CH, POOL_CH), j), _pick((1, D_MODEL), j)]
        scratch += [pltpu.VMEM((TOKEN_TILE + 4 * POOL_HALO, D_MODEL), F32),
                    pltpu.VMEM((TOKEN_TILE + 4 * POOL_HALO, POOL_CH), F32)]
    args += [norm_g, wg, wu, wd]
    specs += [g_spec, _pick((D_MODEL, D_FF), *w_idx), _pick((D_MODEL, D_FF), *w_idx), _pick((D_FF, D_MODEL), *w_idx)]
    if post is not None:
        args.append(g2)
        specs.append(g2_spec)
    out_shape, out_specs = [jax.ShapeDtypeStruct((n, D_MODEL), F32)], [tile]
    if post == "rows":
        rows = TOKEN_TILE // CHUNK
        out_shape.append(jax.ShapeDtypeStruct((N_PART, n // CHUNK, PW), BF16))
        out_specs.append(pl.BlockSpec((N_PART, rows, PW), lambda i: (0, i, 0)))
        scratch.append(pltpu.VMEM((N_SLAB, TOKEN_TILE, SLAB), F32))
    return pl.pallas_call(
        functools.partial(_ffn_body, pre=pre, post=post, tiles_per_seq=seq_len // TOKEN_TILE, seq_len=seq_len),
        grid=(n // TOKEN_TILE,),
        in_specs=specs, out_specs=out_specs, out_shape=out_shape, scratch_shapes=scratch,
        compiler_params=_params(("parallel",)),
    )(*args)


def _cmul(ar, ai, br, bi):
    return ar * br - ai * bi, ar * bi + ai * br


def _dot_nt(a, b):
    return lax.dot_general(a, b, (((1,), (1,)), ((), ())), precision=lax.Precision.HIGHEST, preferred_element_type=F32)


def _s5_prep_part(k, lr_ref, li_ref, ls_ref, br_ref, bi_ref, cr_ref, ci_ref, d_ref, wst_ref, wout_ref, m_ref, a_ref):
    bs, cs = [], []
    for d in range(2):
        lr, li = lr_ref[k, d], li_ref[k, d]
        step = jnp.exp(ls_ref[k, d])
        mag = jnp.exp(lr * step)
        ar, ai = mag * jnp.cos(li * step), mag * jnp.sin(li * step)
        den = lr * lr + li * li
        nr = ar - 1.0
        kr, ki = (nr * lr + ai * li) / den, (ai * lr - nr * li) / den
        bbr, bbi = _cmul(kr, ki, br_ref[k, d], bi_ref[k, d])
        cre, cim = cr_ref[k, d], ci_ref[k, d]
        pr, pi = jnp.ones_like(ar), jnp.zeros_like(ar)
        bs.append([])
        cs.append([])
        for j in range(CHUNK + 1):
            xr, xi = _cmul(pr, pi, bbr, bbi)
            bs[d].append(jnp.concatenate([xr, xi], axis=1))
            yr, yi = _cmul(pr, pi, cre, cim)
            cs[d].append(jnp.concatenate([yr, -yi], axis=1))
            if j == CHUNK:
                a_ref[d, k] = jnp.concatenate([pr, pi], axis=1)
            pr, pi = _cmul(pr, pi, ar, ai)
        for s in range(CHUNK):
            j = CHUNK - 1 - s if d == 0 else s
            wst_ref[d, k, s * PART:(s + 1) * PART, :] = bs[d][j].astype(BF16)
        wout_t = jnp.concatenate([cs[d][t + 1 if d == 0 else CHUNK - t] for t in range(CHUNK)], axis=0)
        wout_ref[d, k] = wout_t.T.astype(BF16)
    kf = _dot_nt(bs[0][0], jnp.concatenate([cs[0][j] for j in range(CHUNK)], axis=0))
    kb = _dot_nt(bs[1][0], jnp.concatenate([cs[1][CHUNK - 1 - j] for j in range(CHUNK)], axis=0))
    ri = lax.broadcasted_iota(jnp.int32, (PART, PW), 0)
    ci = lax.broadcasted_iota(jnp.int32, (PART, PW), 1)
    for s in range(CHUNK):
        f = kf if s == 0 else pltpu.roll(kf, s * PART, 1)
        b = kb if s == CHUNK - 1 else pltpu.roll(kb, (s + 1) * PART, 1)
        blk = (jnp.where(ci >= s * PART, f, 0.0) + jnp.where(ci < (s + 1) * PART, b, 0.0)
               + jnp.where(ci == ri + s * PART, d_ref[k], 0.0))
        m_ref[k, s * PART:(s + 1) * PART, :] = blk.astype(BF16)


def _s5_prep_body(*refs):
    for k in range(PARTS_PER_SLAB):
        _s5_prep_part(k, *refs)


def _s5_prep(lam_re, lam_im, log_step, b_re, b_im, c_re, c_im, d):
    gpp = PART // S5_GROUP
    eye = jnp.eye(gpp, dtype=F32)

    def state_rows(v):
        return v.reshape(2, N_PART, 1, PSTATE).transpose(1, 0, 2, 3)

    def block_diag(v):
        v = v.transpose(1, 0, 2, 3, 4)
        v = v[:, :, :, :, None, :] * eye[None, None, :, None, :, None]
        return v.reshape(N_PART, 2, PART, PSTATE)

    ls = jnp.broadcast_to(log_step[:, :, None], lam_re.shape)
    bshape = (2, N_PART, gpp, S5_STATE, S5_GROUP)
    cshape = (2, N_PART, gpp, S5_GROUP, S5_STATE)
    args = (state_rows(lam_re), state_rows(lam_im), state_rows(ls),
            block_diag(b_re.reshape(bshape).transpose(0, 1, 2, 4, 3)), block_diag(b_im.reshape(bshape).transpose(0, 1, 2, 4, 3)),
            block_diag(c_re.reshape(cshape)), block_diag(c_im.reshape(cshape)),
            jnp.tile(d.reshape(N_PART, 1, PART), (1, 1, CHUNK)))
    pps = PARTS_PER_SLAB
    vec = pl.BlockSpec((pps, 2, 1, PSTATE), lambda c: (c, 0, 0, 0))
    mat = pl.BlockSpec((pps, 2, PART, PSTATE), lambda c: (c, 0, 0, 0))
    wst, wout, m, a = pl.pallas_call(
        _s5_prep_body,
        grid=(N_SLAB,),
        in_specs=[vec, vec, vec, mat, mat, mat, mat, pl.BlockSpec((pps, 1, PW), lambda c: (c, 0, 0))],
        out_specs=[pl.BlockSpec((2, pps, PW, PCOLS), lambda c: (0, c, 0, 0)),
                   pl.BlockSpec((2, pps, PCOLS, PW), lambda c: (0, c, 0, 0)),
                   pl.BlockSpec((pps, PW, PW), lambda c: (c, 0, 0)),
                   pl.BlockSpec((2, pps, 1, PCOLS), lambda c: (0, c, 0, 0))],
        out_shape=[jax.ShapeDtypeStruct((2, N_PART, PW, PCOLS), BF16),
                   jax.ShapeDtypeStruct((2, N_PART, PCOLS, PW), BF16),
                   jax.ShapeDtypeStruct((N_PART, PW, PW), BF16),
                   jax.ShapeDtypeStruct((2, N_PART, 1, PCOLS), F32)],
        compiler_params=_params(("parallel",)),
    )(*args)
    return wst, wout, m, a.reshape(2, N_SET, SUBLANES, PCOLS)


def _s5_states_body(xf_ref, xb_ref, wst_ref, a_ref, of_ref, ob_ref, s_scr, car_scr, *, blocks_per_seq, rows):
    x_refs, o_refs = (xf_ref, xb_ref), (of_ref, ob_ref)
    halves = PCOLS // LANES

    @pl.when(pl.program_id(0) % blocks_per_seq == 0)
    def _():
        car_scr[...] = jnp.zeros_like(car_scr)

    for d in range(2):
        for q in range(N_PART):
            s_loc = jnp.dot(x_refs[d][q], wst_ref[d, q], preferred_element_type=F32)
            for ri in range(halves):
                s_scr[d, halves * (q // SUBLANES) + ri, pl.ds(q % SUBLANES, rows, stride=SUBLANES), :] = (
                    s_loc[:, ri * LANES:(ri + 1) * LANES])

    def step(n, carry):
        new = []
        for d in range(2):
            r = n if d == 0 else rows - 1 - n
            r0 = pl.multiple_of(r * SUBLANES, SUBLANES)
            for st in range(N_SET):
                sr, si = carry[(d * N_SET + st) * 2], carry[(d * N_SET + st) * 2 + 1]
                ar, ai = a_ref[d, st, :, :LANES], a_ref[d, st, :, LANES:]
                xr, xi = s_scr[d, 2 * st, pl.ds(r0, SUBLANES), :], s_scr[d, 2 * st + 1, pl.ds(r0, SUBLANES), :]
                s_scr[d, 2 * st, pl.ds(r0, SUBLANES), :] = sr
                s_scr[d, 2 * st + 1, pl.ds(r0, SUBLANES), :] = si
                new += [ar * sr - ai * si + xr, ar * si + ai * sr + xi]
        return tuple(new)

    n_car = 2 * N_SET * halves
    carry = lax.fori_loop(0, rows, step, tuple(car_scr[k] for k in range(n_car)), unroll=2)
    for k in range(n_car):
        car_scr[k] = carry[k]
    for d in range(2):
        for q in range(N_PART):
            for ri in range(halves):
                o_refs[d][q, :, ri * LANES:(ri + 1) * LANES] = (
                    s_scr[d, halves * (q // SUBLANES) + ri, pl.ds(q % SUBLANES, rows, stride=SUBLANES), :].astype(BF16))


def _s5_states(hs, wst, a, rows_per_seq):
    nrows = hs.shape[1]
    rows = min(ROW_BLOCK, rows_per_seq)
    nb = nrows // rows
    return pl.pallas_call(
        functools.partial(_s5_states_body, blocks_per_seq=rows_per_seq // rows, rows=rows),
        grid=(nb,),
        in_specs=[pl.BlockSpec((N_PART, rows, PW), lambda i: (0, i, 0)),
                  pl.BlockSpec((N_PART, rows, PW), lambda i: (0, nb - 1 - i, 0)),
                  _pick((2, N_PART, PW, PCOLS)), _pick((2, N_SET, SUBLANES, PCOLS))],
        out_specs=[pl.BlockSpec((N_PART, rows, PCOLS), lambda i: (0, i, 0)),
                   pl.BlockSpec((N_PART, rows, PCOLS), lambda i: (0, nb - 1 - i, 0))],
        out_shape=[jax.ShapeDtypeStruct((N_PART, nrows, PCOLS), BF16)] * 2,
        scratch_shapes=[pltpu.VMEM((2, N_SET * (PCOLS // LANES), rows * SUBLANES, LANES), F32),
                        pltpu.VMEM((2 * N_SET * (PCOLS // LANES), SUBLANES, LANES), F32)],
        compiler_params=_params(("arbitrary",)),
    )(hs, hs, wst, a)


def _s5_out_body(x_ref, sf_ref, sb_ref, m_ref, wout_ref, o_ref, *, rows):
    for c in range(N_SLAB):
        ys = []
        for k in range(PARTS_PER_SLAB):
            q = PARTS_PER_SLAB * c + k
            y = jnp.dot(x_ref[q], m_ref[q], preferred_element_type=F32)
            y += jnp.dot(sf_ref[q], wout_ref[0, q], preferred_element_type=F32)
            y += jnp.dot(sb_ref[q], wout_ref[1, q], preferred_element_type=F32)
            ys.append(y)
        for u in range(CHUNK // PARTS_PER_SLAB):
            tok = _swap_blocks([y[:, u * LANES:(u + 1) * LANES] for y in ys])
            for m, piece in enumerate(tok):
                o_ref[c, pl.ds(PARTS_PER_SLAB * u + m, rows, stride=CHUNK), :] = piece


def _s5_out(hs, sf, sb, m, wout, rows_per_seq):
    nrows = hs.shape[1]
    rows = min(ROW_BLOCK, rows_per_seq)
    blk = lambda i: (0, i, 0)
    return pl.pallas_call(
        functools.partial(_s5_out_body, rows=rows),
        grid=(nrows // rows,),
        in_specs=[pl.BlockSpec((N_PART, rows, PW), blk), pl.BlockSpec((N_PART, rows, PCOLS), blk),
                  pl.BlockSpec((N_PART, rows, PCOLS), blk),
                  _pick((N_PART, PW, PW)), _pick((2, N_PART, PCOLS, PW))],
        out_specs=pl.BlockSpec((N_SLAB, rows * CHUNK, SLAB), blk),
        out_shape=jax.ShapeDtypeStruct((N_SLAB, nrows * CHUNK, SLAB), F32),
        compiler_params=_params(("parallel",)),
    )(hs, sf, sb, m, wout)


def _trunk(x, seq_len, norm_g, final_norm_g, wg, wu, wd, s5, s5_wa, s5_wb, pool_w, pool_scale):
    depth = norm_g.shape[0]
    gain = lambda layer, k: _pick((1, D_MODEL), layer, k)
    ffn = functools.partial(_ffn, norm_g=norm_g, wg=wg, wu=wu, wd=wd, seq_len=seq_len)
    for layer in range(depth):
        j = layer // 2
        last = dict(post="final", g2=final_norm_g, g2_spec=_pick((1, D_MODEL))) if layer == depth - 1 else {}
        if layer % 2 == 0:
            wst, wout, m, a = s5[j]
            x, hs = ffn(x, gain(layer, 0), (layer, 0), post="rows", g2=norm_g, g2_spec=gain(layer, 1))
            sf, sb = _s5_states(hs, wst, a, seq_len // CHUNK)
            y = _s5_out(hs, sf, sb, m, wout, seq_len // CHUNK)
            x = ffn(x, gain(layer, 2), (layer, 1), pre="glu", pre_args=(y, s5_wa, s5_wb, j), **last)[0]
        else:
            x = ffn(x, gain(layer, 0), (layer, 0))[0]
            x = ffn(x, gain(layer, 2), (layer, 1), pre="pool", pre_args=(gain(layer, 1), pool_w, pool_scale, j), **last)[0]
    return x


def kernel(x_prompt, x_sample, norm_g, final_norm_g, ffn_w_gate, ffn_w_up, ffn_w_down, s5_lambda_re, s5_lambda_im, s5_log_step, s5_b_re, s5_b_im, s5_c_re, s5_c_im, s5_d, s5_w_glu_a, s5_w_glu_b, pool_w, pool_scale):
    wg, wu, wd = ffn_w_gate.astype(BF16), ffn_w_up.astype(BF16), ffn_w_down.astype(BF16)
    wa, wb, pw = s5_w_glu_a.astype(BF16), s5_w_glu_b.astype(BF16), pool_w.astype(BF16)
    norm_g = norm_g.reshape(norm_g.shape[0], 3, 1, D_MODEL)
    final_norm_g = final_norm_g.reshape(1, D_MODEL)
    pool_scale = pool_scale.reshape(-1, 1, D_MODEL)
    s5 = [_s5_prep(s5_lambda_re[j], s5_lambda_im[j], s5_log_step[j], s5_b_re[j], s5_b_im[j], s5_c_re[j], s5_c_im[j],
                   s5_d[j]) for j in range(s5_lambda_re.shape[0])]
    outs = []
    for x in (x_prompt, x_sample):
        bsz, seq, _ = x.shape
        y = _trunk(x.reshape(bsz * seq, D_MODEL), seq, norm_g, final_norm_g, wg, wu, wd, s5, wa, wb, pw, pool_scale)
        outs.append(y.reshape(bsz, seq, D_MODEL))
    return tuple(outs)
```

```python
import functools

import jax
import jax.numpy as jnp
from jax import lax
from jax.experimental import pallas as pl
from jax.experimental.pallas import tpu as pltpu

D_MODEL = 1024
D_FF = 2816
RMS_EPS = 1e-6
S5_GROUP = 16
S5_STATE = 64
POOL_WINDOWS = (2, 4, 8, 16)
POOL_CH = D_MODEL // len(POOL_WINDOWS)
POOL_HALO = 8

LANES = 128
SUBLANES = 8
SLAB = LANES
N_SLAB = D_MODEL // SLAB
PART = 32
PARTS_PER_SLAB = SLAB // PART
N_PART = D_MODEL // PART
PSTATE = (PART // S5_GROUP) * S5_STATE
PCOLS = 2 * PSTATE
CHUNK = SUBLANES
PW = CHUNK * PART
N_SET = N_PART // SUBLANES
ROW_BLOCK = 128
TOKEN_TILE = 512
VMEM_LIMIT = 56 * 1024 * 1024

F32 = jnp.float32
BF16 = jnp.bfloat16


def _params(sem, vmem=VMEM_LIMIT):
    return pltpu.CompilerParams(dimension_semantics=sem, vmem_limit_bytes=vmem)


def _pick(tail, *lead):
    tail, lead = tuple(tail), tuple(lead)
    return pl.BlockSpec((None,) * len(lead) + tail, lambda *_: lead + (0,) * len(tail), pipeline_mode=pl.Buffered(1))


def _rms(x, g):
    return x * lax.rsqrt(jnp.mean(x * x, axis=-1, keepdims=True) + RMS_EPS) * g


def _tok_pos(t, k):
    return PARTS_PER_SLAB * (t // PARTS_PER_SLAB) + (t % PARTS_PER_SLAB + k) % PARTS_PER_SLAB


def _pos_tok(p, k):
    return PARTS_PER_SLAB * (p // PARTS_PER_SLAB) + (p % PARTS_PER_SLAB - k) % PARTS_PER_SLAB


def _pick_blocks(src, first):
    n = PARTS_PER_SLAB
    blk = lax.broadcasted_iota(jnp.int32, src[0].shape, 1) // PART
    out = src[(first + n - 1) % n]
    for p in range(n - 2, -1, -1):
        out = jnp.where(blk == p, src[(first + p) % n], out)
    return out


def _tok_to_parts(tok):
    n = PARTS_PER_SLAB
    rolled = [tok[0]] + [pltpu.roll(tok[m], m * PART, 1) for m in range(1, n)]
    return [_pick_blocks(rolled, -k) for k in range(n)]


def _parts_to_tok(parts):
    n = PARTS_PER_SLAB
    tok = []
    for m in range(n):
        z = _pick_blocks(parts, -m)
        tok.append(z if m == 0 else pltpu.roll(z, (n - m) * PART, 1))
    return tok


def _pool_mix(x, prev_ref, next_ref, g, w_ref, scale, e_scr, s_scr, j, tiles_per_seq, seq_len):
    h = _rms(x, g)
    lo, hi = 2 * POOL_HALO, 2 * POOL_HALO + TOKEN_TILE
    e_scr[lo - POOL_HALO:lo, :] = jnp.where(j == 0, 0.0, _rms(prev_ref[...], g))
    e_scr[hi:hi + POOL_HALO, :] = jnp.where(j == tiles_per_seq - 1, 0.0, _rms(next_ref[...], g))
    e_scr[lo:hi, :] = h
    pos = j * TOKEN_TILE + lax.broadcasted_iota(jnp.int32, (TOKEN_TILE, 1), 0)
    parts = []
    for gi, win in enumerate(POOL_WINDOWS):
        cols = slice(gi * POOL_CH, (gi + 1) * POOL_CH)
        ext = POOL_HALO - 1
        s_scr[lo - ext:hi + ext, :] = e_scr[lo - ext - 1:hi + ext - 1, cols] + e_scr[lo - ext:hi + ext, cols]
        w = 2
        while w < win:
            half = w // 2
            ext -= half
            s_new = s_scr[lo - ext - half:hi + ext - half, :] + s_scr[lo - ext + half:hi + ext + half, :]
            s_scr[lo - ext:hi + ext, :] = s_new
            w *= 2
        cnt = jnp.minimum(pos + (win - win // 2), seq_len) - jnp.maximum(pos - win // 2, 0)
        p = s_scr[lo:hi, :] / cnt.astype(F32) - h[:, cols]
        parts.append(jnp.dot(p.astype(BF16), w_ref[gi], preferred_element_type=F32))
    return jnp.concatenate(parts, axis=1) * scale


def _ffn_body(*refs, pre, post, tiles_per_seq, seq_len):
    it = iter(refs)
    x_ref = next(it)
    if pre == "glu":
        y_ref, wa_ref, wb_ref = next(it), next(it), next(it)
    if pre == "pool":
        prev_ref, next_ref, pg_ref, pw_ref, ps_ref = next(it), next(it), next(it), next(it), next(it)
    g_ref, wg_ref, wu_ref, wd_ref = next(it), next(it), next(it), next(it)
    if post is not None:
        g2_ref = next(it)
    o_ref = next(it)
    if post == "rows":
        hs_ref = next(it)
    if pre == "pool":
        e_scr, s_scr = next(it), next(it)
    if post == "rows":
        h_scr = next(it)

    x = x_ref[...]
    if pre == "glu":
        y = jnp.concatenate([y_ref[c] for c in range(N_SLAB)], axis=1)
        gy = jax.nn.gelu(y).astype(BF16)
        a = jnp.dot(gy, wa_ref[...], preferred_element_type=F32)
        b = jnp.dot(gy, wb_ref[...], preferred_element_type=F32)
        x = x + a * jax.nn.sigmoid(b)
    if pre == "pool":
        j = pl.program_id(0) % tiles_per_seq
        x = x + _pool_mix(x, prev_ref, next_ref, pg_ref[...], pw_ref, ps_ref[...], e_scr, s_scr, j, tiles_per_seq, seq_len)
    h = _rms(x, g_ref[...]).astype(BF16)
    gate = jnp.dot(h, wg_ref[...], preferred_element_type=F32)
    up = jnp.dot(h, wu_ref[...], preferred_element_type=F32)
    act = (gate * jax.nn.sigmoid(gate) * up).astype(BF16)
    y = x + 0.5 * jnp.dot(act, wd_ref[...], preferred_element_type=F32)
    if post == "final":
        y = _rms(y, g2_ref[...])
    o_ref[...] = y
    if post == "rows":
        h2 = _rms(y, g2_ref[...])
        rows = TOKEN_TILE // CHUNK
        for c in range(N_SLAB):
            h_scr[c] = h2[:, c * SLAB:(c + 1) * SLAB]
        for c in range(N_SLAB):
            for u in range(CHUNK // PARTS_PER_SLAB):
                tok = [h_scr[c, pl.ds(PARTS_PER_SLAB * u + m, rows, stride=CHUNK), :] for m in range(PARTS_PER_SLAB)]
                for k, piece in enumerate(_tok_to_parts(tok)):
                    hs_ref[PARTS_PER_SLAB * c + k, :, u * LANES:(u + 1) * LANES] = piece.astype(BF16)


def _ffn(x, g_spec, w_idx, norm_g, wg, wu, wd, *, seq_len, pre=None, pre_args=(), post=None, g2=None, g2_spec=None):
    n = x.shape[0]
    tile = pl.BlockSpec((TOKEN_TILE, D_MODEL), lambda i: (i, 0))
    args, specs, scratch = [x], [tile], []
    if pre == "glu":
        y, wa, wb, j = pre_args
        args += [y, wa, wb]
        specs += [pl.BlockSpec((N_SLAB, TOKEN_TILE, SLAB), lambda i: (0, i, 0)),
                  _pick((D_MODEL, D_MODEL), j), _pick((D_MODEL, D_MODEL), j)]
    if pre == "pool":
        pg_spec, pw, ps, j = pre_args
        halo_blocks = TOKEN_TILE // POOL_HALO
        last = n // POOL_HALO - 1
        args += [x, x, norm_g, pw, ps]
        specs += [pl.BlockSpec((POOL_HALO, D_MODEL), lambda i: (jnp.maximum(i * halo_blocks - 1, 0), 0)),
                  pl.BlockSpec((POOL_HALO, D_MODEL), lambda i: (jnp.minimum((i + 1) * halo_blocks, last), 0)),
                  pg_spec, _pick((len(POOL_WINDOWS), POOL_CH, POOL_CH), j), _pick((1, D_MODEL), j)]
        scratch += [pltpu.VMEM((TOKEN_TILE + 4 * POOL_HALO, D_MODEL), F32),
                    pltpu.VMEM((TOKEN_TILE + 4 * POOL_HALO, POOL_CH), F32)]
    args += [norm_g, wg, wu, wd]
    specs += [g_spec, _pick((D_MODEL, D_FF), *w_idx), _pick((D_MODEL, D_FF), *w_idx), _pick((D_FF, D_MODEL), *w_idx)]
    if post is not None:
        args.append(g2)
        specs.append(g2_spec)
    out_shape, out_specs = [jax.ShapeDtypeStruct((n, D_MODEL), F32)], [tile]
    if post == "rows":
        rows = TOKEN_TILE // CHUNK
        out_shape.append(jax.ShapeDtypeStruct((N_PART, n // CHUNK, PW), BF16))
        out_specs.append(pl.BlockSpec((N_PART, rows, PW), lambda i: (0, i, 0)))
        scratch.append(pltpu.VMEM((N_SLAB, TOKEN_TILE, SLAB), F32))
    return pl.pallas_call(
        functools.partial(_ffn_body, pre=pre, post=post, tiles_per_seq=seq_len // TOKEN_TILE, seq_len=seq_len),
        grid=(n // TOKEN_TILE,),
        in_specs=specs, out_specs=out_specs, out_shape=out_shape, scratch_shapes=scratch,
        compiler_params=_params(("parallel",)),
    )(*args)


def _cmul(ar, ai, br, bi):
    return ar * br - ai * bi, ar * bi + ai * br


def _dot_nt(a, b):
    return lax.dot_general(a, b, (((1,), (1,)), ((), ())), precision=lax.Precision.HIGHEST, preferred_element_type=F32)


def _s5_prep_part(k, lr_ref, li_ref, ls_ref, br_ref, bi_ref, cr_ref, ci_ref, d_ref, wst_ref, wout_ref, m_ref, a_ref):
    bs, cs = [], []
    for d in range(2):
        lr, li = lr_ref[k, d], li_ref[k, d]
        step = jnp.exp(ls_ref[k, d])
        mag = jnp.exp(lr * step)
        ar, ai = mag * jnp.cos(li * step), mag * jnp.sin(li * step)
        den = lr * lr + li * li
        nr = ar - 1.0
        kr, ki = (nr * lr + ai * li) / den, (ai * lr - nr * li) / den
        bbr, bbi = _cmul(kr, ki, br_ref[k, d], bi_ref[k, d])
        cre, cim = cr_ref[k, d], ci_ref[k, d]
        pr, pi = jnp.ones_like(ar), jnp.zeros_like(ar)
        bs.append([])
        cs.append([])
        for j in range(CHUNK + 1):
            xr, xi = _cmul(pr, pi, bbr, bbi)
            bs[d].append(jnp.concatenate([xr, xi], axis=1))
            yr, yi = _cmul(pr, pi, cre, cim)
            cs[d].append(jnp.concatenate([yr, -yi], axis=1))
            if j == CHUNK:
                a_ref[d, k] = jnp.concatenate([pr, pi], axis=1)
            pr, pi = _cmul(pr, pi, ar, ai)
        toks = [_pos_tok(p, k) for p in range(CHUNK)]
        for p, s in enumerate(toks):
            j = CHUNK - 1 - s if d == 0 else s
            wst_ref[d, k, p * PART:(p + 1) * PART, :] = bs[d][j].astype(BF16)
        wout_t = jnp.concatenate([cs[d][t + 1 if d == 0 else CHUNK - t] for t in toks], axis=0)
        wout_ref[d, k] = wout_t.T.astype(BF16)
    kf = _dot_nt(bs[0][0], jnp.concatenate([cs[0][j] for j in range(CHUNK)], axis=0))
    kb = _dot_nt(bs[1][0], jnp.concatenate([cs[1][CHUNK - 1 - j] for j in range(CHUNK)], axis=0))
    ri = lax.broadcasted_iota(jnp.int32, (PART, PW), 0)
    ci = lax.broadcasted_iota(jnp.int32, (PART, PW), 1)
    for s in range(CHUNK):
        f = kf if s == 0 else pltpu.roll(kf, s * PART, 1)
        b = kb if s == CHUNK - 1 else pltpu.roll(kb, (s + 1) * PART, 1)
        blk = (jnp.where(ci >= s * PART, f, 0.0) + jnp.where(ci < (s + 1) * PART, b, 0.0)
               + jnp.where(ci == ri + s * PART, d_ref[k], 0.0))
        if k:
            blk = jnp.concatenate([pltpu.roll(blk[:, u * LANES:(u + 1) * LANES], k * PART, 1)
                                   for u in range(PW // LANES)], axis=1)
        p = _tok_pos(s, k)
        m_ref[k, p * PART:(p + 1) * PART, :] = blk.astype(BF16)


def _s5_prep_body(*refs):
    for k in range(PARTS_PER_SLAB):
        _s5_prep_part(k, *refs)


def _s5_prep(lam_re, lam_im, log_step, b_re, b_im, c_re, c_im, d):
    gpp = PART // S5_GROUP
    eye = jnp.eye(gpp, dtype=F32)

    def state_rows(v):
        return v.reshape(2, N_PART, 1, PSTATE).transpose(1, 0, 2, 3)

    def block_diag(v):
        v = v.transpose(1, 0, 2, 3, 4)
        v = v[:, :, :, :, None, :] * eye[None, None, :, None, :, None]
        return v.reshape(N_PART, 2, PART, PSTATE)

    ls = jnp.broadcast_to(log_step[:, :, None], lam_re.shape)
    bshape = (2, N_PART, gpp, S5_STATE, S5_GROUP)
    cshape = (2, N_PART, gpp, S5_GROUP, S5_STATE)
    args = (state_rows(lam_re), state_rows(lam_im), state_rows(ls),
            block_diag(b_re.reshape(bshape).transpose(0, 1, 2, 4, 3)), block_diag(b_im.reshape(bshape).transpose(0, 1, 2, 4, 3)),
            block_diag(c_re.reshape(cshape)), block_diag(c_im.reshape(cshape)),
            jnp.tile(d.reshape(N_PART, 1, PART), (1, 1, CHUNK)))
    pps = PARTS_PER_SLAB
    vec = pl.BlockSpec((pps, 2, 1, PSTATE), lambda c: (c, 0, 0, 0))
    mat = pl.BlockSpec((pps, 2, PART, PSTATE), lambda c: (c, 0, 0, 0))
    wst, wout, m, a = pl.pallas_call(
        _s5_prep_body,
        grid=(N_SLAB,),
        in_specs=[vec, vec, vec, mat, mat, mat, mat, pl.BlockSpec((pps, 1, PW), lambda c: (c, 0, 0))],
        out_specs=[pl.BlockSpec((2, pps, PW, PCOLS), lambda c: (0, c, 0, 0)),
                   pl.BlockSpec((2, pps, PCOLS, PW), lambda c: (0, c, 0, 0)),
                   pl.BlockSpec((pps, PW, PW), lambda c: (c, 0, 0)),
                   pl.BlockSpec((2, pps, 1, PCOLS), lambda c: (0, c, 0, 0))],
        out_shape=[jax.ShapeDtypeStruct((2, N_PART, PW, PCOLS), BF16),
                   jax.ShapeDtypeStruct((2, N_PART, PCOLS, PW), BF16),
                   jax.ShapeDtypeStruct((N_PART, PW, PW), BF16),
                   jax.ShapeDtypeStruct((2, N_PART, 1, PCOLS), F32)],
        compiler_params=_params(("parallel",)),
    )(*args)
    return wst, wout, m, a.reshape(2, N_SET, SUBLANES, PCOLS)


def _s5_states_body(xf_ref, xb_ref, wst_ref, a_ref, of_ref, ob_ref, *scr, blocks_per_seq, rows):
    x_refs, o_refs = (xf_ref, xb_ref), (of_ref, ob_ref)
    halves = PCOLS // LANES
    n_piece = N_SET * halves
    car_scr = scr[2 * n_piece]
    piece = lambda d, st, ri: scr[d * n_piece + halves * st + ri]

    @pl.when(pl.program_id(0) % blocks_per_seq == 0)
    def _():
        car_scr[...] = jnp.zeros_like(car_scr)

    for d in range(2):
        for q in range(N_PART):
            s_loc = jnp.dot(x_refs[d][q], wst_ref[d, q], preferred_element_type=F32)
            for ri in range(halves):
                piece(d, q // SUBLANES, ri)[pl.ds(q % SUBLANES, rows, stride=SUBLANES), :] = (
                    s_loc[:, ri * LANES:(ri + 1) * LANES])

    def step(n, carry):
        new = []
        for d in range(2):
            r = n if d == 0 else rows - 1 - n
            r0 = pl.multiple_of(r * SUBLANES, SUBLANES)
            for st in range(N_SET):
                sr, si = carry[(d * N_SET + st) * 2], carry[(d * N_SET + st) * 2 + 1]
                ar, ai = a_ref[d, st, :, :LANES], a_ref[d, st, :, LANES:]
                re_scr, im_scr = piece(d, st, 0), piece(d, st, 1)
                xr, xi = re_scr[pl.ds(r0, SUBLANES), :], im_scr[pl.ds(r0, SUBLANES), :]
                re_scr[pl.ds(r0, SUBLANES), :] = sr
                im_scr[pl.ds(r0, SUBLANES), :] = si
                new += [ar * sr - ai * si + xr, ar * si + ai * sr + xi]
        return tuple(new)

    n_car = 2 * n_piece
    carry = lax.fori_loop(0, rows, step, tuple(car_scr[k] for k in range(n_car)), unroll=2)
    for k in range(n_car):
        car_scr[k] = carry[k]
    for d in range(2):
        for q in range(N_PART):
            for ri in range(halves):
                o_refs[d][q, :, ri * LANES:(ri + 1) * LANES] = (
                    piece(d, q // SUBLANES, ri)[pl.ds(q % SUBLANES, rows, stride=SUBLANES), :].astype(BF16))


def _s5_states(hs, wst, a, rows_per_seq):
    nrows = hs.shape[1]
    rows = min(ROW_BLOCK, rows_per_seq)
    nb = nrows // rows
    n_piece = N_SET * (PCOLS // LANES)
    return pl.pallas_call(
        functools.partial(_s5_states_body, blocks_per_seq=rows_per_seq // rows, rows=rows),
        grid=(nb,),
        in_specs=[pl.BlockSpec((N_PART, rows, PW), lambda i: (0, i, 0)),
                  pl.BlockSpec((N_PART, rows, PW), lambda i: (0, nb - 1 - i, 0)),
                  _pick((2, N_PART, PW, PCOLS)), _pick((2, N_SET, SUBLANES, PCOLS))],
        out_specs=[pl.BlockSpec((N_PART, rows, PCOLS), lambda i: (0, i, 0)),
                   pl.BlockSpec((N_PART, rows, PCOLS), lambda i: (0, nb - 1 - i, 0))],
        out_shape=[jax.ShapeDtypeStruct((N_PART, nrows, PCOLS), BF16)] * 2,
        scratch_shapes=[pltpu.VMEM((rows * SUBLANES, LANES), F32)] * (2 * n_piece)
                       + [pltpu.VMEM((2 * n_piece, SUBLANES, LANES), F32)],
        compiler_params=_params(("arbitrary",)),
    )(hs, hs, wst, a)


def _s5_out_body(x_ref, sf_ref, sb_ref, m_ref, wout_ref, o_ref, *, rows):
    for c in range(N_SLAB):
        ys = []
        for k in range(PARTS_PER_SLAB):
            q = PARTS_PER_SLAB * c + k
            y = jnp.dot(x_ref[q], m_ref[q], preferred_element_type=F32)
            y += jnp.dot(sf_ref[q], wout_ref[0, q], preferred_element_type=F32)
            y += jnp.dot(sb_ref[q], wout_ref[1, q], preferred_element_type=F32)
            ys.append(y)
        for u in range(CHUNK // PARTS_PER_SLAB):
            tok = _parts_to_tok([y[:, u * LANES:(u + 1) * LANES] for y in ys])
            for m, piece in enumerate(tok):
                o_ref[c, pl.ds(PARTS_PER_SLAB * u + m, rows, stride=CHUNK), :] = piece


def _s5_out(hs, sf, sb, m, wout, rows_per_seq):
    nrows = hs.shape[1]
    rows = min(ROW_BLOCK, rows_per_seq)
    blk = lambda i: (0, i, 0)
    return pl.pallas_call(
        functools.partial(_s5_out_body, rows=rows),
        grid=(nrows // rows,),
        in_specs=[pl.BlockSpec((N_PART, rows, PW), blk), pl.BlockSpec((N_PART, rows, PCOLS), blk),
                  pl.BlockSpec((N_PART, rows, PCOLS), blk),
                  _pick((N_PART, PW, PW)), _pick((2, N_PART, PCOLS, PW))],
        out_specs=pl.BlockSpec((N_SLAB, rows * CHUNK, SLAB), blk),
        out_shape=jax.ShapeDtypeStruct((N_SLAB, nrows * CHUNK, SLAB), F32),
        compiler_params=_params(("parallel",)),
    )(hs, sf, sb, m, wout)


def _trunk(x, seq_len, norm_g, final_norm_g, wg, wu, wd, s5, s5_wa, s5_wb, pool_w, pool_scale):
    depth = norm_g.shape[0]
    gain = lambda layer, k: _pick((1, D_MODEL), layer, k)
    ffn = functools.partial(_ffn, norm_g=norm_g, wg=wg, wu=wu, wd=wd, seq_len=seq_len)
    for layer in range(depth):
        j = layer // 2
        last = dict(post="final", g2=final_norm_g, g2_spec=_pick((1, D_MODEL))) if layer == depth - 1 else {}
        if layer % 2 == 0:
            wst, wout, m, a = s5[j]
            x, hs = ffn(x, gain(layer, 0), (layer, 0), post="rows", g2=norm_g, g2_spec=gain(layer, 1))
            sf, sb = _s5_states(hs, wst, a, seq_len // CHUNK)
            y = _s5_out(hs, sf, sb, m, wout, seq_len // CHUNK)
            x = ffn(x, gain(layer, 2), (layer, 1), pre="glu", pre_args=(y, s5_wa, s5_wb, j), **last)[0]
        else:
            x = ffn(x, gain(layer, 0), (layer, 0))[0]
            x = ffn(x, gain(layer, 2), (layer, 1), pre="pool", pre_args=(gain(layer, 1), pool_w, pool_scale, j), **last)[0]
    return x


def kernel(x_prompt, x_sample, norm_g, final_norm_g, ffn_w_gate, ffn_w_up, ffn_w_down, s5_lambda_re, s5_lambda_im, s5_log_step, s5_b_re, s5_b_im, s5_c_re, s5_c_im, s5_d, s5_w_glu_a, s5_w_glu_b, pool_w, pool_scale):
    wg, wu, wd = ffn_w_gate.astype(BF16), ffn_w_up.astype(BF16), ffn_w_down.astype(BF16)
    wa, wb, pw = s5_w_glu_a.astype(BF16), s5_w_glu_b.astype(BF16), pool_w.astype(BF16)
    norm_g = norm_g.reshape(norm_g.shape[0], 3, 1, D_MODEL)
    final_norm_g = final_norm_g.reshape(1, D_MODEL)
    pool_scale = pool_scale.reshape(-1, 1, D_MODEL)
    s5 = [_s5_prep(s5_lambda_re[j], s5_lambda_im[j], s5_log_step[j], s5_b_re[j], s5_b_im[j], s5_c_re[j], s5_c_im[j],
                   s5_d[j]) for j in range(s5_lambda_re.shape[0])]
    outs = []
    for x in (x_prompt, x_sample):
        bsz, seq, _ = x.shape
        y = _trunk(x.reshape(bsz * seq, D_MODEL), seq, norm_g, final_norm_g, wg, wu, wd, s5, wa, wb, pw, pool_scale)
        outs.append(y.reshape(bsz, seq, D_MODEL))
    return tuple(outs)
```

```python
import functools

import jax
import jax.numpy as jnp
from jax import lax
from jax.experimental import pallas as pl
from jax.experimental.pallas import tpu as pltpu

D_MODEL = 1024
D_FF = 2816
RMS_EPS = 1e-6
S5_GROUP = 16
S5_STATE = 64
POOL_WINDOWS = (2, 4, 8, 16)
POOL_CH = D_MODEL // len(POOL_WINDOWS)
POOL_HALO = 8

LANES = 128
SUBLANES = 8
MXU_COLS = 256
SLAB = LANES
N_SLAB = D_MODEL // SLAB
PART = 32
PARTS_PER_SLAB = SLAB // PART
N_PART = D_MODEL // PART
PSTATE = (PART // S5_GROUP) * S5_STATE
PCOLS = 2 * PSTATE
CHUNK = SUBLANES
PW = CHUNK * PART
N_SET = N_PART // SUBLANES
ROW_BLOCK = 128
TOKEN_TILE = 512
VMEM_LIMIT = 56 * 1024 * 1024

F32 = jnp.float32
BF16 = jnp.bfloat16


def _params(sem, vmem=VMEM_LIMIT):
    return pltpu.CompilerParams(dimension_semantics=sem, vmem_limit_bytes=vmem)


def _pick(tail, *lead):
    tail, lead = tuple(tail), tuple(lead)
    return pl.BlockSpec((None,) * len(lead) + tail, lambda *_: lead + (0,) * len(tail), pipeline_mode=pl.Buffered(1))


def _rms(x, g):
    return x * lax.rsqrt(jnp.mean(x * x, axis=-1, keepdims=True) + RMS_EPS) * g


def _tok_pos(t, k):
    return PARTS_PER_SLAB * (t // PARTS_PER_SLAB) + (t % PARTS_PER_SLAB + k) % PARTS_PER_SLAB


def _pos_tok(p, k):
    return PARTS_PER_SLAB * (p // PARTS_PER_SLAB) + (p % PARTS_PER_SLAB - k) % PARTS_PER_SLAB


def _pick_blocks(src, first):
    n = PARTS_PER_SLAB
    blk = lax.broadcasted_iota(jnp.int32, src[0].shape, 1) // PART
    out = src[(first + n - 1) % n]
    for p in range(n - 2, -1, -1):
        out = jnp.where(blk == p, src[(first + p) % n], out)
    return out


def _tok_to_parts(tok):
    n = PARTS_PER_SLAB
    rolled = [tok[0]] + [pltpu.roll(tok[m], m * PART, 1) for m in range(1, n)]
    return [_pick_blocks(rolled, -k) for k in range(n)]


def _parts_to_tok(parts):
    n = PARTS_PER_SLAB
    tok = []
    for m in range(n):
        z = _pick_blocks(parts, -m)
        tok.append(z if m == 0 else pltpu.roll(z, (n - m) * PART, 1))
    return tok


def _pool_pieces(x_ref, prev_ref, next_ref, g, w_ref, scale_ref, e_scr, inv_scr, o_ref, j, tiles_per_seq, seq_len):
    lo, n = 2 * POOL_HALO, TOKEN_TILE + 4 * POOL_HALO
    pad = jnp.zeros((POOL_HALO, D_MODEL), F32)
    e_scr[...] = jnp.concatenate([pad, jnp.where(j == 0, 0.0, _rms(prev_ref[...], g)), _rms(x_ref[...], g),
                                  jnp.where(j == tiles_per_seq - 1, 0.0, _rms(next_ref[...], g)), pad], axis=0)
    pos = j * TOKEN_TILE + lax.broadcasted_iota(jnp.int32, (TOKEN_TILE, 1), 0)
    for gi, win in enumerate(POOL_WINDOWS):
        cnt = jnp.minimum(pos + (win - win // 2), seq_len) - jnp.maximum(pos - win // 2, 0)
        inv_scr[gi] = jnp.broadcast_to(1.0 / cnt.astype(F32), (TOKEN_TILE, LANES))
    yield
    for gi, win in enumerate(POOL_WINDOWS):
        ps = []
        for half in range(POOL_CH // LANES):
            cols = slice(gi * POOL_CH + half * LANES, gi * POOL_CH + (half + 1) * LANES)
            c = e_scr[:, cols]
            w = 1
            while w < win:
                c = c + pltpu.roll(c, w, 0)
                w *= 2
            lead = win - win // 2 - 1
            if lead:
                c = pltpu.roll(c, n - lead, 0)
            ps.append((c[lo:lo + TOKEN_TILE] * inv_scr[gi] - e_scr[lo:lo + TOKEN_TILE, cols]).astype(BF16))
            yield
        cols = slice(gi * POOL_CH, (gi + 1) * POOL_CH)
        z = jnp.dot(jnp.concatenate(ps, axis=1), w_ref[gi], preferred_element_type=F32)
        o_ref[:, cols] = x_ref[:, cols] + z * scale_ref[:, cols]
        yield


def _ffn_core(x, g_ref, wg_ref, wu_ref, wd_ref):
    h = _rms(x, g_ref[...]).astype(BF16)
    gate = jnp.dot(h, wg_ref[...], preferred_element_type=F32)
    up = jnp.dot(h, wu_ref[...], preferred_element_type=F32)
    act = (gate * jax.nn.sigmoid(gate) * up).astype(BF16)
    return x + 0.5 * jnp.dot(act, wd_ref[...], preferred_element_type=F32)


def _ffn_core_with(side, x_ref, g_ref, wg_ref, wu_ref, wd_ref):
    h = _rms(x_ref[...], g_ref[...]).astype(BF16)
    acts = []
    for c in range(D_FF // MXU_COLS):
        cols = slice(c * MXU_COLS, (c + 1) * MXU_COLS)
        gate = jnp.dot(h, wg_ref[:, cols], preferred_element_type=F32)
        up = jnp.dot(h, wu_ref[:, cols], preferred_element_type=F32)
        acts.append((gate * jax.nn.sigmoid(gate) * up).astype(BF16))
        next(side, None)
    act = jnp.concatenate(acts, axis=1)
    ys = []
    for c in range(D_MODEL // MXU_COLS):
        cols = slice(c * MXU_COLS, (c + 1) * MXU_COLS)
        ys.append(x_ref[:, cols] + 0.5 * jnp.dot(act, wd_ref[:, cols], preferred_element_type=F32))
        next(side, None)
    for _ in side:
        pass
    return jnp.concatenate(ys, axis=1)


def _rows_pieces(h_scr, hs_ref):
    rows = TOKEN_TILE // CHUNK
    for c in range(N_SLAB):
        for u in range(CHUNK // PARTS_PER_SLAB):
            tok = [h_scr[c, pl.ds(PARTS_PER_SLAB * u + m, rows, stride=CHUNK), :] for m in range(PARTS_PER_SLAB)]
            for k, piece in enumerate(_tok_to_parts(tok)):
                hs_ref[PARTS_PER_SLAB * c + k, :, u * LANES:(u + 1) * LANES] = piece.astype(BF16)
        yield


def _by_parity(i, stage, buf0, buf1):
    @pl.when(i == 0)
    def _():
        buf1[...] = jnp.zeros_like(buf1)

    @pl.when(i % 2 == 0)
    def _():
        stage(buf0, buf1)

    @pl.when(i % 2 == 1)
    def _():
        stage(buf1, buf0)


def _ffn_body(*refs, pre, post, n_tiles, tiles_per_seq, seq_len):
    it = iter(refs)
    x_ref = next(it)
    if pre == "glu":
        y_ref, wa_ref, wb_ref = next(it), next(it), next(it)
    if pre == "pool":
        prev_ref, next_ref, pg_ref, pw_ref, ps_ref = next(it), next(it), next(it), next(it), next(it)
    g_ref, wg_ref, wu_ref, wd_ref = next(it), next(it), next(it), next(it)
    if post is not None:
        g2_ref = next(it)
    o_ref = next(it)
    if post == "rows":
        hs_ref = next(it)
    if pre == "pool":
        buf0, buf1, e_scr, inv_scr = next(it), next(it), next(it), next(it)
    if post == "rows":
        h_scr = next(it)
    i = pl.program_id(0)
    ffn = functools.partial(_ffn_core, g_ref=g_ref, wg_ref=wg_ref, wu_ref=wu_ref, wd_ref=wd_ref)

    if pre == "pool":
        def stage(mixed_w, mixed_r):
            j = jnp.minimum(i, n_tiles - 1) % tiles_per_seq
            pool = _pool_pieces(x_ref, prev_ref, next_ref, pg_ref[...], pw_ref, ps_ref, e_scr, inv_scr, mixed_w,
                                j, tiles_per_seq, seq_len)
            y = _ffn_core_with(pool, mixed_r, g_ref, wg_ref, wu_ref, wd_ref)
            o_ref[...] = _rms(y, g2_ref[...]) if post == "final" else y
        _by_parity(i, stage, buf0, buf1)
        return

    def x_in():
        x = x_ref[...]
        if pre == "glu":
            y = jnp.concatenate([y_ref[c] for c in range(N_SLAB)], axis=1)
            gy = jax.nn.gelu(y).astype(BF16)
            a = jnp.dot(gy, wa_ref[...], preferred_element_type=F32)
            b = jnp.dot(gy, wb_ref[...], preferred_element_type=F32)
            x = x + a * jax.nn.sigmoid(b)
        return x

    y = ffn(x_in())
    o_ref[...] = _rms(y, g2_ref[...]) if post == "final" else y
    if post == "rows":
        h2 = _rms(y, g2_ref[...])
        for c in range(N_SLAB):
            h_scr[c] = h2[:, c * SLAB:(c + 1) * SLAB]
        for _ in _rows_pieces(h_scr, hs_ref):
            pass


def _ffn(x, g_spec, w_idx, norm_g, wg, wu, wd, *, seq_len, pre=None, pre_args=(), post=None, g2=None, g2_spec=None):
    assert not (pre == "pool" and post == "rows"), "the skewed pool stage has no chunk-row output"
    n = x.shape[0]
    n_tiles = n // TOKEN_TILE
    skewed = pre == "pool"
    cur = (lambda i: jnp.minimum(i, n_tiles - 1)) if skewed else (lambda i: i)
    tile_in = pl.BlockSpec((TOKEN_TILE, D_MODEL), lambda i: (cur(i), 0))
    tile_out = pl.BlockSpec((TOKEN_TILE, D_MODEL), (lambda i: (jnp.maximum(i - 1, 0), 0)) if skewed else (lambda i: (i, 0)))
    args, specs, scratch = [x], [tile_in], []
    if pre == "glu":
        y, wa, wb, j = pre_args
        args += [y, wa, wb]
        specs += [pl.BlockSpec((N_SLAB, TOKEN_TILE, SLAB), lambda i: (0, cur(i), 0)),
                  _pick((D_MODEL, D_MODEL), j), _pick((D_MODEL, D_MODEL), j)]
    if pre == "pool":
        pg_spec, pw, ps, j = pre_args
        halo_blocks = TOKEN_TILE // POOL_HALO
        last = n // POOL_HALO - 1
        args += [x, x, norm_g, pw, ps]
        specs += [pl.BlockSpec((POOL_HALO, D_MODEL), lambda i: (jnp.maximum(cur(i) * halo_blocks - 1, 0), 0)),
                  pl.BlockSpec((POOL_HALO, D_MODEL), lambda i: (jnp.minimum((cur(i) + 1) * halo_blocks, last), 0)),
                  pg_spec, _pick((len(POOL_WINDOWS), POOL_CH, POOL_CH), j), _pick((1, D_MODEL), j)]
        scratch += [pltpu.VMEM((TOKEN_TILE, D_MODEL), F32)] * 2
        scratch += [pltpu.VMEM((TOKEN_TILE + 4 * POOL_HALO, D_MODEL), F32),
                    pltpu.VMEM((len(POOL_WINDOWS), TOKEN_TILE, LANES), F32)]
    args += [norm_g, wg, wu, wd]
    specs += [g_spec, _pick((D_MODEL, D_FF), *w_idx), _pick((D_MODEL, D_FF), *w_idx), _pick((D_FF, D_MODEL), *w_idx)]
    if post is not None:
        args.append(g2)
        specs.append(g2_spec)
    out_shape, out_specs = [jax.ShapeDtypeStruct((n, D_MODEL), F32)], [tile_out]
    if post == "rows":
        rows = TOKEN_TILE // CHUNK
        out_shape.append(jax.ShapeDtypeStruct((N_PART, n // CHUNK, PW), BF16))
        out_specs.append(pl.BlockSpec((N_PART, rows, PW), lambda i: (0, i, 0)))
        scratch.append(pltpu.VMEM((N_SLAB, TOKEN_TILE, SLAB), F32))
    return pl.pallas_call(
        functools.partial(_ffn_body, pre=pre, post=post, n_tiles=n_tiles, tiles_per_seq=seq_len // TOKEN_TILE,
                          seq_len=seq_len),
        grid=(n_tiles + 1 if skewed else n_tiles,),
        in_specs=specs, out_specs=out_specs, out_shape=out_shape, scratch_shapes=scratch,
        compiler_params=_params(("arbitrary",) if skewed else ("parallel",)),
    )(*args)


def _cmul(ar, ai, br, bi):
    return ar * br - ai * bi, ar * bi + ai * br


def _dot_nt(a, b):
    return lax.dot_general(a, b, (((1,), (1,)), ((), ())), precision=lax.Precision.HIGHEST, preferred_element_type=F32)


def _s5_prep_part(k, lr_ref, li_ref, ls_ref, br_ref, bi_ref, cr_ref, ci_ref, d_ref, wst_ref, wout_ref, m_ref, a_ref):
    bs, cs = [], []
    for d in range(2):
        lr, li = lr_ref[k, d], li_ref[k, d]
        step = jnp.exp(ls_ref[k, d])
        mag = jnp.exp(lr * step)
        ar, ai = mag * jnp.cos(li * step), mag * jnp.sin(li * step)
        den = lr * lr + li * li
        nr = ar - 1.0
        kr, ki = (nr * lr + ai * li) / den, (ai * lr - nr * li) / den
        bbr, bbi = _cmul(kr, ki, br_ref[k, d], bi_ref[k, d])
        cre, cim = cr_ref[k, d], ci_ref[k, d]
        pr, pi = jnp.ones_like(ar), jnp.zeros_like(ar)
        bs.append([])
        cs.append([])
        for j in range(CHUNK + 1):
            xr, xi = _cmul(pr, pi, bbr, bbi)
            bs[d].append(jnp.concatenate([xr, xi], axis=1))
            yr, yi = _cmul(pr, pi, cre, cim)
            cs[d].append(jnp.concatenate([yr, -yi], axis=1))
            if j == CHUNK:
                a_ref[d, k] = jnp.concatenate([pr, pi], axis=1)
            pr, pi = _cmul(pr, pi, ar, ai)
        toks = [_pos_tok(p, k) for p in range(CHUNK)]
        for p, s in enumerate(toks):
            j = CHUNK - 1 - s if d == 0 else s
            wst_ref[d, k, p * PART:(p + 1) * PART, :] = bs[d][j].astype(BF16)
        wout_t = jnp.concatenate([cs[d][t + 1 if d == 0 else CHUNK - t] for t in toks], axis=0)
        wout_ref[d, k] = wout_t.T.astype(BF16)
    kf = _dot_nt(bs[0][0], jnp.concatenate([cs[0][j] for j in range(CHUNK)], axis=0))
    kb = _dot_nt(bs[1][0], jnp.concatenate([cs[1][CHUNK - 1 - j] for j in range(CHUNK)], axis=0))
    ri = lax.broadcasted_iota(jnp.int32, (PART, PW), 0)
    ci = lax.broadcasted_iota(jnp.int32, (PART, PW), 1)
    for s in range(CHUNK):
        f = kf if s == 0 else pltpu.roll(kf, s * PART, 1)
        b = kb if s == CHUNK - 1 else pltpu.roll(kb, (s + 1) * PART, 1)
        blk = (jnp.where(ci >= s * PART, f, 0.0) + jnp.where(ci < (s + 1) * PART, b, 0.0)
               + jnp.where(ci == ri + s * PART, d_ref[k], 0.0))
        if k:
            blk = jnp.concatenate([pltpu.roll(blk[:, u * LANES:(u + 1) * LANES], k * PART, 1)
                                   for u in range(PW // LANES)], axis=1)
        p = _tok_pos(s, k)
        m_ref[k, p * PART:(p + 1) * PART, :] = blk.astype(BF16)


def _s5_prep_body(*refs):
    for k in range(PARTS_PER_SLAB):
        _s5_prep_part(k, *refs)


def _s5_prep(lam_re, lam_im, log_step, b_re, b_im, c_re, c_im, d):
    gpp = PART // S5_GROUP
    eye = jnp.eye(gpp, dtype=F32)

    def state_rows(v):
        return v.reshape(2, N_PART, 1, PSTATE).transpose(1, 0, 2, 3)

    def block_diag(v):
        v = v.transpose(1, 0, 2, 3, 4)
        v = v[:, :, :, :, None, :] * eye[None, None, :, None, :, None]
        return v.reshape(N_PART, 2, PART, PSTATE)

    ls = jnp.broadcast_to(log_step[:, :, None], lam_re.shape)
    bshape = (2, N_PART, gpp, S5_STATE, S5_GROUP)
    cshape = (2, N_PART, gpp, S5_GROUP, S5_STATE)
    args = (state_rows(lam_re), state_rows(lam_im), state_rows(ls),
            block_diag(b_re.reshape(bshape).transpose(0, 1, 2, 4, 3)), block_diag(b_im.reshape(bshape).transpose(0, 1, 2, 4, 3)),
            block_diag(c_re.reshape(cshape)), block_diag(c_im.reshape(cshape)),
            jnp.tile(d.reshape(N_PART, 1, PART), (1, 1, CHUNK)))
    pps = PARTS_PER_SLAB
    vec = pl.BlockSpec((pps, 2, 1, PSTATE), lambda c: (c, 0, 0, 0))
    mat = pl.BlockSpec((pps, 2, PART, PSTATE), lambda c: (c, 0, 0, 0))
    wst, wout, m, a = pl.pallas_call(
        _s5_prep_body,
        grid=(N_SLAB,),
        in_specs=[vec, vec, vec, mat, mat, mat, mat, pl.BlockSpec((pps, 1, PW), lambda c: (c, 0, 0))],
        out_specs=[pl.BlockSpec((2, pps, PW, PCOLS), lambda c: (0, c, 0, 0)),
                   pl.BlockSpec((2, pps, PCOLS, PW), lambda c: (0, c, 0, 0)),
                   pl.BlockSpec((pps, PW, PW), lambda c: (c, 0, 0)),
                   pl.BlockSpec((2, pps, 1, PCOLS), lambda c: (0, c, 0, 0))],
        out_shape=[jax.ShapeDtypeStruct((2, N_PART, PW, PCOLS), BF16),
                   jax.ShapeDtypeStruct((2, N_PART, PCOLS, PW), BF16),
                   jax.ShapeDtypeStruct((N_PART, PW, PW), BF16),
                   jax.ShapeDtypeStruct((2, N_PART, 1, PCOLS), F32)],
        compiler_params=_params(("parallel",)),
    )(*args)
    return wst, wout, m, a.reshape(2, N_SET, SUBLANES, PCOLS)


def _s5_states_body(xf_ref, xb_ref, wst_ref, a_ref, of_ref, ob_ref, *scr, blocks_per_seq, rows):
    x_refs, o_refs = (xf_ref, xb_ref), (of_ref, ob_ref)
    halves = PCOLS // LANES
    n_piece = N_SET * halves
    car_scr = scr[2 * n_piece]
    piece = lambda d, st, ri: scr[d * n_piece + halves * st + ri]

    @pl.when(pl.program_id(0) % blocks_per_seq == 0)
    def _():
        car_scr[...] = jnp.zeros_like(car_scr)

    for d in range(2):
        for q in range(N_PART):
            s_loc = jnp.dot(x_refs[d][q], wst_ref[d, q], preferred_element_type=F32)
            for ri in range(halves):
                piece(d, q // SUBLANES, ri)[pl.ds(q % SUBLANES, rows, stride=SUBLANES), :] = (
                    s_loc[:, ri * LANES:(ri + 1) * LANES])

    def step(n, carry):
        new = []
        for d in range(2):
            r = n if d == 0 else rows - 1 - n
            r0 = pl.multiple_of(r * SUBLANES, SUBLANES)
            for st in range(N_SET):
                sr, si = carry[(d * N_SET + st) * 2], carry[(d * N_SET + st) * 2 + 1]
                ar, ai = a_ref[d, st, :, :LANES], a_ref[d, st, :, LANES:]
                re_scr, im_scr = piece(d, st, 0), piece(d, st, 1)
                xr, xi = re_scr[pl.ds(r0, SUBLANES), :], im_scr[pl.ds(r0, SUBLANES), :]
                re_scr[pl.ds(r0, SUBLANES), :] = sr
                im_scr[pl.ds(r0, SUBLANES), :] = si
                new += [ar * sr - ai * si + xr, ar * si + ai * sr + xi]
        return tuple(new)

    n_car = 2 * n_piece
    carry = lax.fori_loop(0, rows, step, tuple(car_scr[k] for k in range(n_car)), unroll=2)
    for k in range(n_car):
        car_scr[k] = carry[k]
    for d in range(2):
        for q in range(N_PART):
            for ri in range(halves):
                o_refs[d][q, :, ri * LANES:(ri + 1) * LANES] = (
                    piece(d, q // SUBLANES, ri)[pl.ds(q % SUBLANES, rows, stride=SUBLANES), :].astype(BF16))


def _s5_states(hs, wst, a, rows_per_seq):
    nrows = hs.shape[1]
    rows = min(ROW_BLOCK, rows_per_seq)
    nb = nrows // rows
    n_piece = N_SET * (PCOLS // LANES)
    return pl.pallas_call(
        functools.partial(_s5_states_body, blocks_per_seq=rows_per_seq // rows, rows=rows),
        grid=(nb,),
        in_specs=[pl.BlockSpec((N_PART, rows, PW), lambda i: (0, i, 0)),
                  pl.BlockSpec((N_PART, rows, PW), lambda i: (0, nb - 1 - i, 0)),
                  _pick((2, N_PART, PW, PCOLS)), _pick((2, N_SET, SUBLANES, PCOLS))],
        out_specs=[pl.BlockSpec((N_PART, rows, PCOLS), lambda i: (0, i, 0)),
                   pl.BlockSpec((N_PART, rows, PCOLS), lambda i: (0, nb - 1 - i, 0))],
        out_shape=[jax.ShapeDtypeStruct((N_PART, nrows, PCOLS), BF16)] * 2,
        scratch_shapes=[pltpu.VMEM((rows * SUBLANES, LANES), F32)] * (2 * n_piece)
                       + [pltpu.VMEM((2 * n_piece, SUBLANES, LANES), F32)],
        compiler_params=_params(("arbitrary",)),
    )(hs, hs, wst, a)


def _s5_out_body(x_ref, sf_ref, sb_ref, m_ref, wout_ref, o_ref, *, rows):
    for c in range(N_SLAB):
        ys = []
        for k in range(PARTS_PER_SLAB):
            q = PARTS_PER_SLAB * c + k
            y = jnp.dot(x_ref[q], m_ref[q], preferred_element_type=F32)
            y += jnp.dot(sf_ref[q], wout_ref[0, q], preferred_element_type=F32)
            y += jnp.dot(sb_ref[q], wout_ref[1, q], preferred_element_type=F32)
            ys.append(y)
        for u in range(CHUNK // PARTS_PER_SLAB):
            tok = _parts_to_tok([y[:, u * LANES:(u + 1) * LANES] for y in ys])
            for m, piece in enumerate(tok):
                o_ref[c, pl.ds(PARTS_PER_SLAB * u + m, rows, stride=CHUNK), :] = piece


def _s5_out(hs, sf, sb, m, wout, rows_per_seq):
    nrows = hs.shape[1]
    rows = min(ROW_BLOCK, rows_per_seq)
    blk = lambda i: (0, i, 0)
    return pl.pallas_call(
        functools.partial(_s5_out_body, rows=rows),
        grid=(nrows // rows,),
        in_specs=[pl.BlockSpec((N_PART, rows, PW), blk), pl.BlockSpec((N_PART, rows, PCOLS), blk),
                  pl.BlockSpec((N_PART, rows, PCOLS), blk),
                  _pick((N_PART, PW, PW)), _pick((2, N_PART, PCOLS, PW))],
        out_specs=pl.BlockSpec((N_SLAB, rows * CHUNK, SLAB), blk),
        out_shape=jax.ShapeDtypeStruct((N_SLAB, nrows * CHUNK, SLAB), F32),
        compiler_params=_params(("parallel",)),
    )(hs, sf, sb, m, wout)


def _trunk(x, seq_len, norm_g, final_norm_g, wg, wu, wd, s5, s5_wa, s5_wb, pool_w, pool_scale):
    depth = norm_g.shape[0]
    gain = lambda layer, k: _pick((1, D_MODEL), layer, k)
    ffn = functools.partial(_ffn, norm_g=norm_g, wg=wg, wu=wu, wd=wd, seq_len=seq_len)
    for layer in range(depth):
        j = layer // 2
        last = dict(post="final", g2=final_norm_g, g2_spec=_pick((1, D_MODEL))) if layer == depth - 1 else {}
        if layer % 2 == 0:
            wst, wout, m, a = s5[j]
            x, hs = ffn(x, gain(layer, 0), (layer, 0), post="rows", g2=norm_g, g2_spec=gain(layer, 1))
            sf, sb = _s5_states(hs, wst, a, seq_len // CHUNK)
            y = _s5_out(hs, sf, sb, m, wout, seq_len // CHUNK)
            x = ffn(x, gain(layer, 2), (layer, 1), pre="glu", pre_args=(y, s5_wa, s5_wb, j), **last)[0]
        else:
            x = ffn(x, gain(layer, 0), (layer, 0))[0]
            x = ffn(x, gain(layer, 2), (layer, 1), pre="pool", pre_args=(gain(layer, 1), pool_w, pool_scale, j), **last)[0]
    return x


def kernel(x_prompt, x_sample, norm_g, final_norm_g, ffn_w_gate, ffn_w_up, ffn_w_down, s5_lambda_re, s5_lambda_im, s5_log_step, s5_b_re, s5_b_im, s5_c_re, s5_c_im, s5_d, s5_w_glu_a, s5_w_glu_b, pool_w, pool_scale):
    wg, wu, wd = ffn_w_gate.astype(BF16), ffn_w_up.astype(BF16), ffn_w_down.astype(BF16)
    wa, wb, pw = s5_w_glu_a.astype(BF16), s5_w_glu_b.astype(BF16), pool_w.astype(BF16)
    norm_g = norm_g.reshape(norm_g.shape[0], 3, 1, D_MODEL)
    final_norm_g = final_norm_g.reshape(1, D_MODEL)
    pool_scale = pool_scale.reshape(-1, 1, D_MODEL)
    s5 = [_s5_prep(s5_lambda_re[j], s5_lambda_im[j], s5_log_step[j], s5_b_re[j], s5_b_im[j], s5_c_re[j], s5_c_im[j],
                   s5_d[j]) for j in range(s5_lambda_re.shape[0])]
    outs = []
    for x in (x_prompt, x_sample):
        bsz, seq, _ = x.shape
        y = _trunk(x.reshape(bsz * seq, D_MODEL), seq, norm_g, final_norm_g, wg, wu, wd, s5, wa, wb, pw, pool_scale)
        outs.append(y.reshape(bsz, seq, D_MODEL))
    return tuple(outs)
```

```python
import functools

import jax
import jax.numpy as jnp
from jax import lax
from jax.experimental import pallas as pl
from jax.experimental.pallas import tpu as pltpu

D_MODEL = 1024
D_FF = 2816
RMS_EPS = 1e-6
S5_GROUP = 16
S5_STATE = 64
POOL_WINDOWS = (2, 4, 8, 16)
POOL_CH = D_MODEL // len(POOL_WINDOWS)
POOL_HALO = 8

LANES = 128
SUBLANES = 8
MXU_COLS = 256
SLAB = LANES
N_SLAB = D_MODEL // SLAB
PART = 32
PARTS_PER_SLAB = SLAB // PART
N_PART = D_MODEL // PART
PSTATE = (PART // S5_GROUP) * S5_STATE
PCOLS = 2 * PSTATE
CHUNK = SUBLANES
PW = CHUNK * PART
N_SET = N_PART // SUBLANES
ROW_BLOCK = 128
TOKEN_TILE = 512
VMEM_LIMIT = 56 * 1024 * 1024

F32 = jnp.float32
BF16 = jnp.bfloat16


def _params(sem, vmem=VMEM_LIMIT):
    return pltpu.CompilerParams(dimension_semantics=sem, vmem_limit_bytes=vmem)


def _pick(tail, *lead):
    tail, lead = tuple(tail), tuple(lead)
    return pl.BlockSpec((None,) * len(lead) + tail, lambda *_: lead + (0,) * len(tail), pipeline_mode=pl.Buffered(1))


def _rms(x, g):
    return x * lax.rsqrt(jnp.mean(x * x, axis=-1, keepdims=True) + RMS_EPS) * g


def _tok_pos(t, k):
    return PARTS_PER_SLAB * (t // PARTS_PER_SLAB) + (t % PARTS_PER_SLAB + k) % PARTS_PER_SLAB


def _pos_tok(p, k):
    return PARTS_PER_SLAB * (p // PARTS_PER_SLAB) + (p % PARTS_PER_SLAB - k) % PARTS_PER_SLAB


def _pick_blocks(src, first):
    n = PARTS_PER_SLAB
    blk = lax.broadcasted_iota(jnp.int32, src[0].shape, 1) // PART
    out = src[(first + n - 1) % n]
    for p in range(n - 2, -1, -1):
        out = jnp.where(blk == p, src[(first + p) % n], out)
    return out


def _tok_to_parts(tok):
    n = PARTS_PER_SLAB
    rolled = [tok[0]] + [pltpu.roll(tok[m], m * PART, 1) for m in range(1, n)]
    return [_pick_blocks(rolled, -k) for k in range(n)]


def _parts_to_tok(parts):
    n = PARTS_PER_SLAB
    tok = []
    for m in range(n):
        z = _pick_blocks(parts, -m)
        tok.append(z if m == 0 else pltpu.roll(z, (n - m) * PART, 1))
    return tok


def _pool_pieces(x_ref, prev_ref, next_ref, g, w_ref, scale_ref, e_scr, inv_scr, o_ref, j, tiles_per_seq, seq_len):
    lo, n = 2 * POOL_HALO, TOKEN_TILE + 4 * POOL_HALO
    pad = jnp.zeros((POOL_HALO, D_MODEL), F32)
    e_scr[...] = jnp.concatenate([pad, jnp.where(j == 0, 0.0, _rms(prev_ref[...], g)), _rms(x_ref[...], g),
                                  jnp.where(j == tiles_per_seq - 1, 0.0, _rms(next_ref[...], g)), pad], axis=0)
    pos = j * TOKEN_TILE + lax.broadcasted_iota(jnp.int32, (TOKEN_TILE, 1), 0)
    for gi, win in enumerate(POOL_WINDOWS):
        cnt = jnp.minimum(pos + (win - win // 2), seq_len) - jnp.maximum(pos - win // 2, 0)
        inv_scr[gi] = jnp.broadcast_to(1.0 / cnt.astype(F32), (TOKEN_TILE, LANES))
    yield
    for gi, win in enumerate(POOL_WINDOWS):
        ps = []
        for half in range(POOL_CH // LANES):
            cols = slice(gi * POOL_CH + half * LANES, gi * POOL_CH + (half + 1) * LANES)
            c = e_scr[:, cols]
            w = 1
            while w < win:
                c = c + pltpu.roll(c, w, 0)
                w *= 2
            lead = win - win // 2 - 1
            if lead:
                c = pltpu.roll(c, n - lead, 0)
            ps.append((c[lo:lo + TOKEN_TILE] * inv_scr[gi] - e_scr[lo:lo + TOKEN_TILE, cols]).astype(BF16))
            yield
        cols = slice(gi * POOL_CH, (gi + 1) * POOL_CH)
        z = jnp.dot(jnp.concatenate(ps, axis=1), w_ref[gi], preferred_element_type=F32)
        o_ref[:, cols] = x_ref[:, cols] + z * scale_ref[:, cols]
        yield


def _ffn_core(x, g_ref, wg_ref, wu_ref, wd_ref):
    h = _rms(x, g_ref[...]).astype(BF16)
    gate = jnp.dot(h, wg_ref[...], preferred_element_type=F32)
    up = jnp.dot(h, wu_ref[...], preferred_element_type=F32)
    act = (gate * jax.nn.sigmoid(gate) * up).astype(BF16)
    return x + 0.5 * jnp.dot(act, wd_ref[...], preferred_element_type=F32)


def _ffn_core_with(side, x_ref, g_ref, wg_ref, wu_ref, wd_ref):
    h = _rms(x_ref[...], g_ref[...]).astype(BF16)
    acts = []
    for c in range(D_FF // MXU_COLS):
        cols = slice(c * MXU_COLS, (c + 1) * MXU_COLS)
        gate = jnp.dot(h, wg_ref[:, cols], preferred_element_type=F32)
        up = jnp.dot(h, wu_ref[:, cols], preferred_element_type=F32)
        acts.append((gate * jax.nn.sigmoid(gate) * up).astype(BF16))
        next(side, None)
    act = jnp.concatenate(acts, axis=1)
    ys = []
    for c in range(D_MODEL // MXU_COLS):
        cols = slice(c * MXU_COLS, (c + 1) * MXU_COLS)
        ys.append(x_ref[:, cols] + 0.5 * jnp.dot(act, wd_ref[:, cols], preferred_element_type=F32))
        next(side, None)
    for _ in side:
        pass
    return jnp.concatenate(ys, axis=1)


def _rows_pieces(h_scr, hs_ref):
    rows = TOKEN_TILE // CHUNK
    for c in range(N_SLAB):
        for u in range(CHUNK // PARTS_PER_SLAB):
            tok = [h_scr[c, pl.ds(PARTS_PER_SLAB * u + m, rows, stride=CHUNK), :] for m in range(PARTS_PER_SLAB)]
            for k, piece in enumerate(_tok_to_parts(tok)):
                hs_ref[PARTS_PER_SLAB * c + k, :, u * LANES:(u + 1) * LANES] = piece.astype(BF16)
        yield


def _by_parity(i, stage, buf0, buf1):
    @pl.when(i == 0)
    def _():
        buf1[...] = jnp.zeros_like(buf1)

    @pl.when(i % 2 == 0)
    def _():
        stage(buf0, buf1)

    @pl.when(i % 2 == 1)
    def _():
        stage(buf1, buf0)


def _ffn_body(*refs, pre, post, n_tiles, tiles_per_seq, seq_len):
    it = iter(refs)
    x_ref = next(it)
    if pre == "glu":
        y_ref, wa_ref, wb_ref = next(it), next(it), next(it)
    if pre == "pool":
        prev_ref, next_ref, pg_ref, pw_ref, ps_ref = next(it), next(it), next(it), next(it), next(it)
    g_ref, wg_ref, wu_ref, wd_ref = next(it), next(it), next(it), next(it)
    if post is not None:
        g2_ref = next(it)
    o_ref = next(it)
    if post == "rows":
        hs_ref = next(it)
    if pre == "pool":
        buf0, buf1, e_scr, inv_scr = next(it), next(it), next(it), next(it)
    if post == "rows":
        h_scr = next(it)
    i = pl.program_id(0)
    ffn = functools.partial(_ffn_core, g_ref=g_ref, wg_ref=wg_ref, wu_ref=wu_ref, wd_ref=wd_ref)

    if pre == "pool":
        def stage(mixed_w, mixed_r):
            j = jnp.minimum(i, n_tiles - 1) % tiles_per_seq
            pool = _pool_pieces(x_ref, prev_ref, next_ref, pg_ref[...], pw_ref, ps_ref, e_scr, inv_scr, mixed_w,
                                j, tiles_per_seq, seq_len)
            y = _ffn_core_with(pool, mixed_r, g_ref, wg_ref, wu_ref, wd_ref)
            o_ref[...] = _rms(y, g2_ref[...]) if post == "final" else y
        _by_parity(i, stage, buf0, buf1)
        return

    def x_in():
        x = x_ref[...]
        if pre == "glu":
            y = jnp.concatenate([y_ref[c] for c in range(N_SLAB)], axis=1)
            gy = jax.nn.gelu(y).astype(BF16)
            a = jnp.dot(gy, wa_ref[...], preferred_element_type=F32)
            b = jnp.dot(gy, wb_ref[...], preferred_element_type=F32)
            x = x + a * jax.nn.sigmoid(b)
        return x

    y = ffn(x_in())
    o_ref[...] = _rms(y, g2_ref[...]) if post == "final" else y
    if post == "rows":
        h2 = _rms(y, g2_ref[...])
        for c in range(N_SLAB):
            h_scr[c] = h2[:, c * SLAB:(c + 1) * SLAB]
        for _ in _rows_pieces(h_scr, hs_ref):
            pass


def _ffn(x, g_spec, w_idx, norm_g, wg, wu, wd, *, seq_len, pre=None, pre_args=(), post=None, g2=None, g2_spec=None):
    assert not (pre == "pool" and post == "rows"), "the skewed pool stage has no chunk-row output"
    n = x.shape[0]
    n_tiles = n // TOKEN_TILE
    skewed = pre == "pool"
    cur = (lambda i: jnp.minimum(i, n_tiles - 1)) if skewed else (lambda i: i)
    tile_in = pl.BlockSpec((TOKEN_TILE, D_MODEL), lambda i: (cur(i), 0))
    tile_out = pl.BlockSpec((TOKEN_TILE, D_MODEL), (lambda i: (jnp.maximum(i - 1, 0), 0)) if skewed else (lambda i: (i, 0)))
    args, specs, scratch = [x], [tile_in], []
    if pre == "glu":
        y, wa, wb, j = pre_args
        args += [y, wa, wb]
        specs += [pl.BlockSpec((N_SLAB, TOKEN_TILE, SLAB), lambda i: (0, cur(i), 0)),
                  _pick((D_MODEL, D_MODEL), j), _pick((D_MODEL, D_MODEL), j)]
    if pre == "pool":
        pg_spec, pw, ps, j = pre_args
        halo_blocks = TOKEN_TILE // POOL_HALO
        last = n // POOL_HALO - 1
        args += [x, x, norm_g, pw, ps]
        specs += [pl.BlockSpec((POOL_HALO, D_MODEL), lambda i: (jnp.maximum(cur(i) * halo_blocks - 1, 0), 0)),
                  pl.BlockSpec((POOL_HALO, D_MODEL), lambda i: (jnp.minimum((cur(i) + 1) * halo_blocks, last), 0)),
                  pg_spec, _pick((len(POOL_WINDOWS), POOL_CH, POOL_CH), j), _pick((1, D_MODEL), j)]
        scratch += [pltpu.VMEM((TOKEN_TILE, D_MODEL), F32)] * 2
        scratch += [pltpu.VMEM((TOKEN_TILE + 4 * POOL_HALO, D_MODEL), F32),
                    pltpu.VMEM((len(POOL_WINDOWS), TOKEN_TILE, LANES), F32)]
    args += [norm_g, wg, wu, wd]
    specs += [g_spec, _pick((D_MODEL, D_FF), *w_idx), _pick((D_MODEL, D_FF), *w_idx), _pick((D_FF, D_MODEL), *w_idx)]
    if post is not None:
        args.append(g2)
        specs.append(g2_spec)
    out_shape, out_specs = [jax.ShapeDtypeStruct((n, D_MODEL), F32)], [tile_out]
    if post == "rows":
        rows = TOKEN_TILE // CHUNK
        out_shape.append(jax.ShapeDtypeStruct((N_PART, n // CHUNK, PW), BF16))
        out_specs.append(pl.BlockSpec((N_PART, rows, PW), lambda i: (0, i, 0)))
        scratch.append(pltpu.VMEM((N_SLAB, TOKEN_TILE, SLAB), F32))
    return pl.pallas_call(
        functools.partial(_ffn_body, pre=pre, post=post, n_tiles=n_tiles, tiles_per_seq=seq_len // TOKEN_TILE,
                          seq_len=seq_len),
        grid=(n_tiles + 1 if skewed else n_tiles,),
        in_specs=specs, out_specs=out_specs, out_shape=out_shape, scratch_shapes=scratch,
        compiler_params=_params(("arbitrary",) if skewed else ("parallel",)),
    )(*args)


def _cmul(ar, ai, br, bi):
    return ar * br - ai * bi, ar * bi + ai * br


def _dot_nt(a, b):
    return lax.dot_general(a, b, (((1,), (1,)), ((), ())), precision=lax.Precision.HIGHEST, preferred_element_type=F32)


def _s5_prep_part(k, lr_ref, li_ref, ls_ref, br_ref, bi_ref, cr_ref, ci_ref, d_ref, wst_ref, wout_ref, m_ref, a_ref):
    bs, cs = [], []
    for d in range(2):
        lr, li = lr_ref[k, d], li_ref[k, d]
        step = jnp.exp(ls_ref[k, d])
        mag = jnp.exp(lr * step)
        ar, ai = mag * jnp.cos(li * step), mag * jnp.sin(li * step)
        den = lr * lr + li * li
        nr = ar - 1.0
        kr, ki = (nr * lr + ai * li) / den, (ai * lr - nr * li) / den
        bbr, bbi = _cmul(kr, ki, br_ref[k, d], bi_ref[k, d])
        cre, cim = cr_ref[k, d], ci_ref[k, d]
        pr, pi = jnp.ones_like(ar), jnp.zeros_like(ar)
        bs.append([])
        cs.append([])
        for j in range(CHUNK + 1):
            xr, xi = _cmul(pr, pi, bbr, bbi)
            bs[d].append(jnp.concatenate([xr, xi], axis=1))
            yr, yi = _cmul(pr, pi, cre, cim)
            cs[d].append(jnp.concatenate([yr, -yi], axis=1))
            if j == CHUNK:
                a_ref[d, k] = jnp.concatenate([pr, pi], axis=1)
            pr, pi = _cmul(pr, pi, ar, ai)
        toks = [_pos_tok(p, k) for p in range(CHUNK)]
        for p, s in enumerate(toks):
            j = CHUNK - 1 - s if d == 0 else s
            wst_ref[d, k, p * PART:(p + 1) * PART, :] = bs[d][j].astype(BF16)
        wout_t = jnp.concatenate([cs[d][t + 1 if d == 0 else CHUNK - t] for t in toks], axis=0)
        wout_ref[d, k] = wout_t.T.astype(BF16)
    kf = _dot_nt(bs[0][0], jnp.concatenate([cs[0][j] for j in range(CHUNK)], axis=0))
    kb = _dot_nt(bs[1][0], jnp.concatenate([cs[1][CHUNK - 1 - j] for j in range(CHUNK)], axis=0))
    ri = lax.broadcasted_iota(jnp.int32, (PART, PW), 0)
    ci = lax.broadcasted_iota(jnp.int32, (PART, PW), 1)
    for s in range(CHUNK):
        f = kf if s == 0 else pltpu.roll(kf, s * PART, 1)
        b = kb if s == CHUNK - 1 else pltpu.roll(kb, (s + 1) * PART, 1)
        blk = (jnp.where(ci >= s * PART, f, 0.0) + jnp.where(ci < (s + 1) * PART, b, 0.0)
               + jnp.where(ci == ri + s * PART, d_ref[k], 0.0))
        if k:
            blk = jnp.concatenate([pltpu.roll(blk[:, u * LANES:(u + 1) * LANES], k * PART, 1)
                                   for u in range(PW // LANES)], axis=1)
        p = _tok_pos(s, k)
        m_ref[k, p * PART:(p + 1) * PART, :] = blk.astype(BF16)


def _s5_prep_body(*refs):
    for k in range(PARTS_PER_SLAB):
        _s5_prep_part(k, *refs)


def _s5_prep(lam_re, lam_im, log_step, b_re, b_im, c_re, c_im, d):
    gpp = PART // S5_GROUP
    eye = jnp.eye(gpp, dtype=F32)

    def state_rows(v):
        return v.reshape(2, N_PART, 1, PSTATE).transpose(1, 0, 2, 3)

    def block_diag(v):
        v = v.transpose(1, 0, 2, 3, 4)
        v = v[:, :, :, :, None, :] * eye[None, None, :, None, :, None]
        return v.reshape(N_PART, 2, PART, PSTATE)

    ls = jnp.broadcast_to(log_step[:, :, None], lam_re.shape)
    bshape = (2, N_PART, gpp, S5_STATE, S5_GROUP)
    cshape = (2, N_PART, gpp, S5_GROUP, S5_STATE)
    args = (state_rows(lam_re), state_rows(lam_im), state_rows(ls),
            block_diag(b_re.reshape(bshape).transpose(0, 1, 2, 4, 3)), block_diag(b_im.reshape(bshape).transpose(0, 1, 2, 4, 3)),
            block_diag(c_re.reshape(cshape)), block_diag(c_im.reshape(cshape)),
            jnp.tile(d.reshape(N_PART, 1, PART), (1, 1, CHUNK)))
    pps = PARTS_PER_SLAB
    vec = pl.BlockSpec((pps, 2, 1, PSTATE), lambda c: (c, 0, 0, 0))
    mat = pl.BlockSpec((pps, 2, PART, PSTATE), lambda c: (c, 0, 0, 0))
    wst, wout, m, a = pl.pallas_call(
        _s5_prep_body,
        grid=(N_SLAB,),
        in_specs=[vec, vec, vec, mat, mat, mat, mat, pl.BlockSpec((pps, 1, PW), lambda c: (c, 0, 0))],
        out_specs=[pl.BlockSpec((2, pps, PW, PCOLS), lambda c: (0, c, 0, 0)),
                   pl.BlockSpec((2, pps, PCOLS, PW), lambda c: (0, c, 0, 0)),
                   pl.BlockSpec((pps, PW, PW), lambda c: (c, 0, 0)),
                   pl.BlockSpec((2, pps, 1, PCOLS), lambda c: (0, c, 0, 0))],
        out_shape=[jax.ShapeDtypeStruct((2, N_PART, PW, PCOLS), BF16),
                   jax.ShapeDtypeStruct((2, N_PART, PCOLS, PW), BF16),
                   jax.ShapeDtypeStruct((N_PART, PW, PW), BF16),
                   jax.ShapeDtypeStruct((2, N_PART, 1, PCOLS), F32)],
        compiler_params=_params(("parallel",)),
    )(*args)
    return wst, wout, m, a.reshape(2, N_SET, SUBLANES, PCOLS)


def _s5_states_body(xf_ref, xb_ref, wst_ref, a_ref, of_ref, ob_ref, *scr, blocks_per_seq, rows):
    x_refs, o_refs = (xf_ref, xb_ref), (of_ref, ob_ref)
    halves = PCOLS // LANES
    n_piece = N_SET * halves
    car_scr = scr[2 * n_piece]
    piece = lambda d, st, ri: scr[d * n_piece + halves * st + ri]

    @pl.when(pl.program_id(0) % blocks_per_seq == 0)
    def _():
        car_scr[...] = jnp.zeros_like(car_scr)

    for d in range(2):
        for q in range(N_PART):
            s_loc = jnp.dot(x_refs[d][q], wst_ref[d, q], preferred_element_type=F32)
            for ri in range(halves):
                piece(d, q // SUBLANES, ri)[pl.ds(q % SUBLANES, rows, stride=SUBLANES), :] = (
                    s_loc[:, ri * LANES:(ri + 1) * LANES])

    def step(n, carry):
        new = []
        for d in range(2):
            r = n if d == 0 else rows - 1 - n
            r0 = pl.multiple_of(r * SUBLANES, SUBLANES)
            for st in range(N_SET):
                sr, si = carry[(d * N_SET + st) * 2], carry[(d * N_SET + st) * 2 + 1]
                ar, ai = a_ref[d, st, :, :LANES], a_ref[d, st, :, LANES:]
                re_scr, im_scr = piece(d, st, 0), piece(d, st, 1)
                xr, xi = re_scr[pl.ds(r0, SUBLANES), :], im_scr[pl.ds(r0, SUBLANES), :]
                re_scr[pl.ds(r0, SUBLANES), :] = sr
                im_scr[pl.ds(r0, SUBLANES), :] = si
                new += [ar * sr - ai * si + xr, ar * si + ai * sr + xi]
        return tuple(new)

    n_car = 2 * n_piece
    carry = lax.fori_loop(0, rows, step, tuple(car_scr[k] for k in range(n_car)), unroll=4)
    for k in range(n_car):
        car_scr[k] = carry[k]
    for d in range(2):
        for q in range(N_PART):
            for ri in range(halves):
                o_refs[d][q, :, ri * LANES:(ri + 1) * LANES] = (
                    piece(d, q // SUBLANES, ri)[pl.ds(q % SUBLANES, rows, stride=SUBLANES), :].astype(BF16))


def _s5_states(hs, wst, a, rows_per_seq):
    nrows = hs.shape[1]
    rows = min(ROW_BLOCK, rows_per_seq)
    nb = nrows // rows
    n_piece = N_SET * (PCOLS // LANES)
    return pl.pallas_call(
        functools.partial(_s5_states_body, blocks_per_seq=rows_per_seq // rows, rows=rows),
        grid=(nb,),
        in_specs=[pl.BlockSpec((N_PART, rows, PW), lambda i: (0, i, 0)),
                  pl.BlockSpec((N_PART, rows, PW), lambda i: (0, nb - 1 - i, 0)),
                  _pick((2, N_PART, PW, PCOLS)), _pick((2, N_SET, SUBLANES, PCOLS))],
        out_specs=[pl.BlockSpec((N_PART, rows, PCOLS), lambda i: (0, i, 0)),
                   pl.BlockSpec((N_PART, rows, PCOLS), lambda i: (0, nb - 1 - i, 0))],
        out_shape=[jax.ShapeDtypeStruct((N_PART, nrows, PCOLS), BF16)] * 2,
        scratch_shapes=[pltpu.VMEM((rows * SUBLANES, LANES), F32)] * (2 * n_piece)
                       + [pltpu.VMEM((2 * n_piece, SUBLANES, LANES), F32)],
        compiler_params=_params(("arbitrary",)),
    )(hs, hs, wst, a)


def _s5_out_body(x_ref, sf_ref, sb_ref, m_ref, wout_ref, o_ref, *, rows):
    for c in range(N_SLAB):
        ys = []
        for k in range(PARTS_PER_SLAB):
            q = PARTS_PER_SLAB * c + k
            y = jnp.dot(x_ref[q], m_ref[q], preferred_element_type=F32)
            y += jnp.dot(sf_ref[q], wout_ref[0, q], preferred_element_type=F32)
            y += jnp.dot(sb_ref[q], wout_ref[1, q], preferred_element_type=F32)
            ys.append(y)
        for u in range(CHUNK // PARTS_PER_SLAB):
            tok = _parts_to_tok([y[:, u * LANES:(u + 1) * LANES] for y in ys])
            for m, piece in enumerate(tok):
                o_ref[c, pl.ds(PARTS_PER_SLAB * u + m, rows, stride=CHUNK), :] = piece


def _s5_out(hs, sf, sb, m, wout, rows_per_seq):
    nrows = hs.shape[1]
    rows = min(ROW_BLOCK, rows_per_seq)
    blk = lambda i: (0, i, 0)
    return pl.pallas_call(
        functools.partial(_s5_out_body, rows=rows),
        grid=(nrows // rows,),
        in_specs=[pl.BlockSpec((N_PART, rows, PW), blk), pl.BlockSpec((N_PART, rows, PCOLS), blk),
                  pl.BlockSpec((N_PART, rows, PCOLS), blk),
                  _pick((N_PART, PW, PW)), _pick((2, N_PART, PCOLS, PW))],
        out_specs=pl.BlockSpec((N_SLAB, rows * CHUNK, SLAB), blk),
        out_shape=jax.ShapeDtypeStruct((N_SLAB, nrows * CHUNK, SLAB), F32),
        compiler_params=_params(("parallel",)),
    )(hs, sf, sb, m, wout)


def _trunk(x, seq_len, norm_g, final_norm_g, wg, wu, wd, s5, s5_wa, s5_wb, pool_w, pool_scale):
    depth = norm_g.shape[0]
    gain = lambda layer, k: _pick((1, D_MODEL), layer, k)
    ffn = functools.partial(_ffn, norm_g=norm_g, wg=wg, wu=wu, wd=wd, seq_len=seq_len)
    for layer in range(depth):
        j = layer // 2
        last = dict(post="final", g2=final_norm_g, g2_spec=_pick((1, D_MODEL))) if layer == depth - 1 else {}
        if layer % 2 == 0:
            wst, wout, m, a = s5[j]
            x, hs = ffn(x, gain(layer, 0), (layer, 0), post="rows", g2=norm_g, g2_spec=gain(layer, 1))
            sf, sb = _s5_states(hs, wst, a, seq_len // CHUNK)
            y = _s5_out(hs, sf, sb, m, wout, seq_len // CHUNK)
            x = ffn(x, gain(layer, 2), (layer, 1), pre="glu", pre_args=(y, s5_wa, s5_wb, j), **last)[0]
        else:
            x = ffn(x, gain(layer, 0), (layer, 0))[0]
            x = ffn(x, gain(layer, 2), (layer, 1), pre="pool", pre_args=(gain(layer, 1), pool_w, pool_scale, j), **last)[0]
    return x


def kernel(x_prompt, x_sample, norm_g, final_norm_g, ffn_w_gate, ffn_w_up, ffn_w_down, s5_lambda_re, s5_lambda_im, s5_log_step, s5_b_re, s5_b_im, s5_c_re, s5_c_im, s5_d, s5_w_glu_a, s5_w_glu_b, pool_w, pool_scale):
    wg, wu, wd = ffn_w_gate.astype(BF16), ffn_w_up.astype(BF16), ffn_w_down.astype(BF16)
    wa, wb, pw = s5_w_glu_a.astype(BF16), s5_w_glu_b.astype(BF16), pool_w.astype(BF16)
    norm_g = norm_g.reshape(norm_g.shape[0], 3, 1, D_MODEL)
    final_norm_g = final_norm_g.reshape(1, D_MODEL)
    pool_scale = pool_scale.reshape(-1, 1, D_MODEL)
    s5 = [_s5_prep(s5_lambda_re[j], s5_lambda_im[j], s5_log_step[j], s5_b_re[j], s5_b_im[j], s5_c_re[j], s5_c_im[j],
                   s5_d[j]) for j in range(s5_lambda_re.shape[0])]
    outs = []
    for x in (x_prompt, x_sample):
        bsz, seq, _ = x.shape
        y = _trunk(x.reshape(bsz * seq, D_MODEL), seq, norm_g, final_norm_g, wg, wu, wd, s5, wa, wb, pw, pool_scale)
        outs.append(y.reshape(bsz, seq, D_MODEL))
    return tuple(outs)
```

```python
import functools

import jax
import jax.numpy as jnp
from jax import lax
from jax.experimental import pallas as pl
from jax.experimental.pallas import tpu as pltpu

D_MODEL = 1024
D_FF = 2816
RMS_EPS = 1e-6
S5_GROUP = 16
S5_STATE = 64
POOL_WINDOWS = (2, 4, 8, 16)
POOL_CH = D_MODEL // len(POOL_WINDOWS)
POOL_HALO = 8

LANES = 128
SUBLANES = 8
BF16_ROWS = 16
MXU_COLS = 256
SLAB = LANES
N_SLAB = D_MODEL // SLAB
PART = 32
PARTS_PER_SLAB = SLAB // PART
N_PART = D_MODEL // PART
PSTATE = (PART // S5_GROUP) * S5_STATE
PCOLS = 2 * PSTATE
CHUNK = SUBLANES
PW = CHUNK * PART
N_SET = N_PART // SUBLANES
ROW_BLOCK = 128
TOKEN_TILE = 512
VMEM_LIMIT = 56 * 1024 * 1024

F32 = jnp.float32
BF16 = jnp.bfloat16


def _params(sem, vmem=VMEM_LIMIT):
    return pltpu.CompilerParams(dimension_semantics=sem, vmem_limit_bytes=vmem)


def _pick(tail, *lead):
    tail, lead = tuple(tail), tuple(lead)
    return pl.BlockSpec((None,) * len(lead) + tail, lambda *_: lead + (0,) * len(tail), pipeline_mode=pl.Buffered(1))


def _rms(x, g):
    return x * lax.rsqrt(jnp.mean(x * x, axis=-1, keepdims=True) + RMS_EPS) * g


def _tok_pos(t, k):
    return PARTS_PER_SLAB * (t // PARTS_PER_SLAB) + (t % PARTS_PER_SLAB + k) % PARTS_PER_SLAB


def _pos_tok(p, k):
    return PARTS_PER_SLAB * (p // PARTS_PER_SLAB) + (p % PARTS_PER_SLAB - k) % PARTS_PER_SLAB


def _pick_blocks(src, first):
    n = PARTS_PER_SLAB
    blk = lax.broadcasted_iota(jnp.int32, src[0].shape, 1) // PART
    out = src[(first + n - 1) % n]
    for p in range(n - 2, -1, -1):
        out = jnp.where(blk == p, src[(first + p) % n], out)
    return out


def _tok_to_parts(tok):
    n = PARTS_PER_SLAB
    rolled = [tok[0]] + [pltpu.roll(tok[m], m * PART, 1) for m in range(1, n)]
    return [_pick_blocks(rolled, -k) for k in range(n)]


def _parts_to_tok(parts):
    n = PARTS_PER_SLAB
    tok = []
    for m in range(n):
        z = _pick_blocks(parts, -m)
        tok.append(z if m == 0 else pltpu.roll(z, (n - m) * PART, 1))
    return tok


def _pool_pieces(x_ref, prev_ref, next_ref, g, w_ref, scale_ref, e_scr, inv_scr, o_ref, j, tiles_per_seq, seq_len):
    lo, n = 2 * POOL_HALO, TOKEN_TILE + 4 * POOL_HALO
    pad = jnp.zeros((POOL_HALO, D_MODEL), F32)
    e_scr[...] = jnp.concatenate([pad, jnp.where(j == 0, 0.0, _rms(prev_ref[...], g)), _rms(x_ref[...], g),
                                  jnp.where(j == tiles_per_seq - 1, 0.0, _rms(next_ref[...], g)), pad], axis=0)
    pos = j * TOKEN_TILE + lax.broadcasted_iota(jnp.int32, (TOKEN_TILE, 1), 0)
    for gi, win in enumerate(POOL_WINDOWS):
        cnt = jnp.minimum(pos + (win - win // 2), seq_len) - jnp.maximum(pos - win // 2, 0)
        inv_scr[gi] = jnp.broadcast_to(1.0 / cnt.astype(F32), (TOKEN_TILE, LANES))
    yield
    for gi, win in enumerate(POOL_WINDOWS):
        ps = []
        for half in range(POOL_CH // LANES):
            cols = slice(gi * POOL_CH + half * LANES, gi * POOL_CH + (half + 1) * LANES)
            c = e_scr[:, cols]
            w = 1
            while w < win:
                c = c + pltpu.roll(c, w, 0)
                w *= 2
            lead = win - win // 2 - 1
            if lead:
                c = pltpu.roll(c, n - lead, 0)
            ps.append((c[lo:lo + TOKEN_TILE] * inv_scr[gi] - e_scr[lo:lo + TOKEN_TILE, cols]).astype(BF16))
            yield
        cols = slice(gi * POOL_CH, (gi + 1) * POOL_CH)
        z = jnp.dot(jnp.concatenate(ps, axis=1), w_ref[gi], preferred_element_type=F32)
        o_ref[:, cols] = x_ref[:, cols] + z * scale_ref[:, cols]
        yield


def _ffn_core(x, g_ref, wg_ref, wu_ref, wd_ref, between=None):
    h = _rms(x, g_ref[...]).astype(BF16)
    gate = jnp.dot(h, wg_ref[...], preferred_element_type=F32)
    if between is not None:
        between()
    up = jnp.dot(h, wu_ref[...], preferred_element_type=F32)
    act = (gate * jax.nn.sigmoid(gate) * up).astype(BF16)
    return x + 0.5 * jnp.dot(act, wd_ref[...], preferred_element_type=F32)


def _ffn_core_with(side, x_ref, g_ref, wg_ref, wu_ref, wd_ref):
    h = _rms(x_ref[...], g_ref[...]).astype(BF16)
    acts = []
    for c in range(D_FF // MXU_COLS):
        cols = slice(c * MXU_COLS, (c + 1) * MXU_COLS)
        gate = jnp.dot(h, wg_ref[:, cols], preferred_element_type=F32)
        up = jnp.dot(h, wu_ref[:, cols], preferred_element_type=F32)
        acts.append((gate * jax.nn.sigmoid(gate) * up).astype(BF16))
        next(side, None)
    act = jnp.concatenate(acts, axis=1)
    ys = []
    for c in range(D_MODEL // MXU_COLS):
        cols = slice(c * MXU_COLS, (c + 1) * MXU_COLS)
        ys.append(x_ref[:, cols] + 0.5 * jnp.dot(act, wd_ref[:, cols], preferred_element_type=F32))
        next(side, None)
    for _ in side:
        pass
    return jnp.concatenate(ys, axis=1)


def _rows_pieces(h_scr, hs_ref):
    rows = TOKEN_TILE // CHUNK
    for c in range(N_SLAB):
        for u in range(CHUNK // PARTS_PER_SLAB):
            tok = [h_scr[c, pl.ds(PARTS_PER_SLAB * u + m, rows, stride=CHUNK), :] for m in range(PARTS_PER_SLAB)]
            for k, piece in enumerate(_tok_to_parts(tok)):
                hs_ref[PARTS_PER_SLAB * c + k, :, u * LANES:(u + 1) * LANES] = piece.astype(BF16)
        yield


def _by_parity(i, stage, buf0, buf1):
    @pl.when(i == 0)
    def _():
        buf1[...] = jnp.zeros_like(buf1)

    @pl.when(i % 2 == 0)
    def _():
        stage(buf0, buf1)

    @pl.when(i % 2 == 1)
    def _():
        stage(buf1, buf0)


def _ffn_body(*refs, pre, post, n_cast, n_tiles, tiles_per_seq, seq_len):
    it = iter(refs)
    x_ref = next(it)
    if pre == "glu":
        y_ref, wa_ref, wb_ref = next(it), next(it), next(it)
    if pre == "pool":
        prev_ref, next_ref, pg_ref, pw_ref, ps_ref = next(it), next(it), next(it), next(it), next(it)
    g_ref, wg_ref, wu_ref, wd_ref = next(it), next(it), next(it), next(it)
    if post is not None:
        g2_ref = next(it)
    cast_in = [next(it) for _ in range(n_cast)]
    o_ref = next(it)
    if post == "rows":
        hs_ref = next(it)
    cast_out = [next(it) for _ in range(n_cast)]
    if pre == "pool":
        buf0, buf1, e_scr, inv_scr = next(it), next(it), next(it), next(it)
    if post == "rows":
        h_scr = next(it)
    i = pl.program_id(0)

    def cast_slabs():
        for src, dst in zip(cast_in, cast_out):
            dst[...] = src[...].astype(BF16)

    ffn = functools.partial(_ffn_core, g_ref=g_ref, wg_ref=wg_ref, wu_ref=wu_ref, wd_ref=wd_ref,
                            between=cast_slabs if n_cast else None)

    if pre == "pool":
        def stage(mixed_w, mixed_r):
            j = jnp.minimum(i, n_tiles - 1) % tiles_per_seq
            pool = _pool_pieces(x_ref, prev_ref, next_ref, pg_ref[...], pw_ref, ps_ref, e_scr, inv_scr, mixed_w,
                                j, tiles_per_seq, seq_len)
            y = _ffn_core_with(pool, mixed_r, g_ref, wg_ref, wu_ref, wd_ref)
            o_ref[...] = _rms(y, g2_ref[...]) if post == "final" else y
        _by_parity(i, stage, buf0, buf1)
        return

    def x_in():
        x = x_ref[...]
        if pre == "glu":
            y = jnp.concatenate([y_ref[c] for c in range(N_SLAB)], axis=1)
            gy = jax.nn.gelu(y).astype(BF16)
            a = jnp.dot(gy, wa_ref[...], preferred_element_type=F32)
            b = jnp.dot(gy, wb_ref[...], preferred_element_type=F32)
            x = x + a * jax.nn.sigmoid(b)
        return x

    y = ffn(x_in())
    o_ref[...] = _rms(y, g2_ref[...]) if post == "final" else y
    if post == "rows":
        h2 = _rms(y, g2_ref[...])
        for c in range(N_SLAB):
            h_scr[c] = h2[:, c * SLAB:(c + 1) * SLAB]
        for _ in _rows_pieces(h_scr, hs_ref):
            pass


def _ffn(x, g_spec, w_idx, norm_g, wg, wu, wd, *, seq_len, pre=None, pre_args=(), post=None, g2=None, g2_spec=None,
         cast=()):
    assert not (pre == "pool" and (post == "rows" or cast)), "the skewed pool stage has no extra outputs"
    n = x.shape[0]
    n_tiles = n // TOKEN_TILE
    skewed = pre == "pool"
    cur = (lambda i: jnp.minimum(i, n_tiles - 1)) if skewed else (lambda i: i)
    tile_in = pl.BlockSpec((TOKEN_TILE, D_MODEL), lambda i: (cur(i), 0))
    tile_out = pl.BlockSpec((TOKEN_TILE, D_MODEL), (lambda i: (jnp.maximum(i - 1, 0), 0)) if skewed else (lambda i: (i, 0)))
    args, specs, scratch = [x], [tile_in], []
    if pre == "glu":
        y, wa, wb, j = pre_args
        args += [y, wa, wb]
        specs += [pl.BlockSpec((N_SLAB, TOKEN_TILE, SLAB), lambda i: (0, cur(i), 0)),
                  _pick((D_MODEL, D_MODEL), j), _pick((D_MODEL, D_MODEL), j)]
    if pre == "pool":
        pg_spec, pw, ps, j = pre_args
        halo_blocks = TOKEN_TILE // POOL_HALO
        last = n // POOL_HALO - 1
        args += [x, x, norm_g, pw, ps]
        specs += [pl.BlockSpec((POOL_HALO, D_MODEL), lambda i: (jnp.maximum(cur(i) * halo_blocks - 1, 0), 0)),
                  pl.BlockSpec((POOL_HALO, D_MODEL), lambda i: (jnp.minimum((cur(i) + 1) * halo_blocks, last), 0)),
                  pg_spec, _pick((len(POOL_WINDOWS), POOL_CH, POOL_CH), j), _pick((1, D_MODEL), j)]
        scratch += [pltpu.VMEM((TOKEN_TILE, D_MODEL), F32)] * 2
        scratch += [pltpu.VMEM((TOKEN_TILE + 4 * POOL_HALO, D_MODEL), F32),
                    pltpu.VMEM((len(POOL_WINDOWS), TOKEN_TILE, LANES), F32)]
    args += [norm_g, wg, wu, wd]
    specs += [g_spec, _pick((D_MODEL, D_FF), *w_idx), _pick((D_MODEL, D_FF), *w_idx), _pick((D_FF, D_MODEL), *w_idx)]
    if post is not None:
        args.append(g2)
        specs.append(g2_spec)
    out_shape, out_specs = [jax.ShapeDtypeStruct((n, D_MODEL), F32)], [tile_out]
    if post == "rows":
        rows = TOKEN_TILE // CHUNK
        out_shape.append(jax.ShapeDtypeStruct((N_PART, n // CHUNK, PW), BF16))
        out_specs.append(pl.BlockSpec((N_PART, rows, PW), lambda i: (0, i, 0)))
        scratch.append(pltpu.VMEM((N_SLAB, TOKEN_TILE, SLAB), F32))
    for w in cast:
        slab = pl.BlockSpec((w.shape[0] // n_tiles, w.shape[1]), lambda i: (i, 0))
        assert w.shape[0] % (n_tiles * BF16_ROWS) == 0
        args.append(w)
        specs.append(slab)
        out_shape.append(jax.ShapeDtypeStruct(w.shape, BF16))
        out_specs.append(slab)
    return pl.pallas_call(
        functools.partial(_ffn_body, pre=pre, post=post, n_cast=len(cast), n_tiles=n_tiles,
                          tiles_per_seq=seq_len // TOKEN_TILE, seq_len=seq_len),
        grid=(n_tiles + 1 if skewed else n_tiles,),
        in_specs=specs, out_specs=out_specs, out_shape=out_shape, scratch_shapes=scratch,
        compiler_params=_params(("arbitrary",) if skewed else ("parallel",)),
    )(*args)


def _cmul(ar, ai, br, bi):
    return ar * br - ai * bi, ar * bi + ai * br


def _dot_nt(a, b):
    return lax.dot_general(a, b, (((1,), (1,)), ((), ())), precision=lax.Precision.HIGHEST, preferred_element_type=F32)


def _s5_prep_part(k, lr_ref, li_ref, ls_ref, br_ref, bi_ref, cr_ref, ci_ref, d_ref, wst_ref, wout_ref, m_ref, a_ref):
    bs, cs = [], []
    for d in range(2):
        lr, li = lr_ref[k, d], li_ref[k, d]
        step = jnp.exp(ls_ref[k, d])
        mag = jnp.exp(lr * step)
        ar, ai = mag * jnp.cos(li * step), mag * jnp.sin(li * step)
        den = lr * lr + li * li
        nr = ar - 1.0
        kr, ki = (nr * lr + ai * li) / den, (ai * lr - nr * li) / den
        bbr, bbi = _cmul(kr, ki, br_ref[k, d], bi_ref[k, d])
        cre, cim = cr_ref[k, d], ci_ref[k, d]
        pr, pi = jnp.ones_like(ar), jnp.zeros_like(ar)
        bs.append([])
        cs.append([])
        for j in range(CHUNK + 1):
            xr, xi = _cmul(pr, pi, bbr, bbi)
            bs[d].append(jnp.concatenate([xr, xi], axis=1))
            yr, yi = _cmul(pr, pi, cre, cim)
            cs[d].append(jnp.concatenate([yr, -yi], axis=1))
            if j == CHUNK:
                a_ref[d, k] = jnp.concatenate([pr, pi], axis=1)
            pr, pi = _cmul(pr, pi, ar, ai)
        toks = [_pos_tok(p, k) for p in range(CHUNK)]
        for p, s in enumerate(toks):
            j = CHUNK - 1 - s if d == 0 else s
            wst_ref[d, k, p * PART:(p + 1) * PART, :] = bs[d][j].astype(BF16)
        wout_t = jnp.concatenate([cs[d][t + 1 if d == 0 else CHUNK - t] for t in toks], axis=0)
        wout_ref[d, k] = wout_t.T.astype(BF16)
    kf = _dot_nt(bs[0][0], jnp.concatenate([cs[0][j] for j in range(CHUNK)], axis=0))
    kb = _dot_nt(bs[1][0], jnp.concatenate([cs[1][CHUNK - 1 - j] for j in range(CHUNK)], axis=0))
    ri = lax.broadcasted_iota(jnp.int32, (PART, PW), 0)
    ci = lax.broadcasted_iota(jnp.int32, (PART, PW), 1)
    for s in range(CHUNK):
        f = kf if s == 0 else pltpu.roll(kf, s * PART, 1)
        b = kb if s == CHUNK - 1 else pltpu.roll(kb, (s + 1) * PART, 1)
        blk = (jnp.where(ci >= s * PART, f, 0.0) + jnp.where(ci < (s + 1) * PART, b, 0.0)
               + jnp.where(ci == ri + s * PART, d_ref[k], 0.0))
        if k:
            blk = jnp.concatenate([pltpu.roll(blk[:, u * LANES:(u + 1) * LANES], k * PART, 1)
                                   for u in range(PW // LANES)], axis=1)
        p = _tok_pos(s, k)
        m_ref[k, p * PART:(p + 1) * PART, :] = blk.astype(BF16)


def _s5_prep_body(*refs):
    for k in range(PARTS_PER_SLAB):
        _s5_prep_part(k, *refs)


def _s5_prep(lam_re, lam_im, log_step, b_re, b_im, c_re, c_im, d):
    gpp = PART // S5_GROUP
    eye = jnp.eye(gpp, dtype=F32)

    def state_rows(v):
        return v.reshape(2, N_PART, 1, PSTATE).transpose(1, 0, 2, 3)

    def block_diag(v):
        v = v.transpose(1, 0, 2, 3, 4)
        v = v[:, :, :, :, None, :] * eye[None, None, :, None, :, None]
        return v.reshape(N_PART, 2, PART, PSTATE)

    ls = jnp.broadcast_to(log_step[:, :, None], lam_re.shape)
    bshape = (2, N_PART, gpp, S5_STATE, S5_GROUP)
    cshape = (2, N_PART, gpp, S5_GROUP, S5_STATE)
    args = (state_rows(lam_re), state_rows(lam_im), state_rows(ls),
            block_diag(b_re.reshape(bshape).transpose(0, 1, 2, 4, 3)), block_diag(b_im.reshape(bshape).transpose(0, 1, 2, 4, 3)),
            block_diag(c_re.reshape(cshape)), block_diag(c_im.reshape(cshape)),
            jnp.tile(d.reshape(N_PART, 1, PART), (1, 1, CHUNK)))
    pps = PARTS_PER_SLAB
    vec = pl.BlockSpec((pps, 2, 1, PSTATE), lambda c: (c, 0, 0, 0))
    mat = pl.BlockSpec((pps, 2, PART, PSTATE), lambda c: (c, 0, 0, 0))
    wst, wout, m, a = pl.pallas_call(
        _s5_prep_body,
        grid=(N_SLAB,),
        in_specs=[vec, vec, vec, mat, mat, mat, mat, pl.BlockSpec((pps, 1, PW), lambda c: (c, 0, 0))],
        out_specs=[pl.BlockSpec((2, pps, PW, PCOLS), lambda c: (0, c, 0, 0)),
                   pl.BlockSpec((2, pps, PCOLS, PW), lambda c: (0, c, 0, 0)),
                   pl.BlockSpec((pps, PW, PW), lambda c: (c, 0, 0)),
                   pl.BlockSpec((2, pps, 1, PCOLS), lambda c: (0, c, 0, 0))],
        out_shape=[jax.ShapeDtypeStruct((2, N_PART, PW, PCOLS), BF16),
                   jax.ShapeDtypeStruct((2, N_PART, PCOLS, PW), BF16),
                   jax.ShapeDtypeStruct((N_PART, PW, PW), BF16),
                   jax.ShapeDtypeStruct((2, N_PART, 1, PCOLS), F32)],
        compiler_params=_params(("parallel",)),
    )(*args)
    return wst, wout, m, a.reshape(2, N_SET, SUBLANES, PCOLS)


def _s5_states_body(xf_ref, xb_ref, wst_ref, a_ref, of_ref, ob_ref, *scr, blocks_per_seq, rows):
    x_refs, o_refs = (xf_ref, xb_ref), (of_ref, ob_ref)
    halves = PCOLS // LANES
    n_piece = N_SET * halves
    car_scr = scr[2 * n_piece]
    piece = lambda d, st, ri: scr[d * n_piece + halves * st + ri]

    @pl.when(pl.program_id(0) % blocks_per_seq == 0)
    def _():
        car_scr[...] = jnp.zeros_like(car_scr)

    for d in range(2):
        for q in range(N_PART):
            s_loc = jnp.dot(x_refs[d][q], wst_ref[d, q], preferred_element_type=F32)
            for ri in range(halves):
                piece(d, q // SUBLANES, ri)[pl.ds(q % SUBLANES, rows, stride=SUBLANES), :] = (
                    s_loc[:, ri * LANES:(ri + 1) * LANES])

    def step(n, carry):
        new = []
        for d in range(2):
            r = n if d == 0 else rows - 1 - n
            r0 = pl.multiple_of(r * SUBLANES, SUBLANES)
            for st in range(N_SET):
                sr, si = carry[(d * N_SET + st) * 2], carry[(d * N_SET + st) * 2 + 1]
                ar, ai = a_ref[d, st, :, :LANES], a_ref[d, st, :, LANES:]
                re_scr, im_scr = piece(d, st, 0), piece(d, st, 1)
                xr, xi = re_scr[pl.ds(r0, SUBLANES), :], im_scr[pl.ds(r0, SUBLANES), :]
                re_scr[pl.ds(r0, SUBLANES), :] = sr
                im_scr[pl.ds(r0, SUBLANES), :] = si
                new += [ar * sr - ai * si + xr, ar * si + ai * sr + xi]
        return tuple(new)

    n_car = 2 * n_piece
    carry = lax.fori_loop(0, rows, step, tuple(car_scr[k] for k in range(n_car)), unroll=4)
    for k in range(n_car):
        car_scr[k] = carry[k]
    for d in range(2):
        for q in range(N_PART):
            for ri in range(halves):
                o_refs[d][q, :, ri * LANES:(ri + 1) * LANES] = (
                    piece(d, q // SUBLANES, ri)[pl.ds(q % SUBLANES, rows, stride=SUBLANES), :].astype(BF16))


def _s5_states(hs, wst, a, rows_per_seq):
    nrows = hs.shape[1]
    rows = min(ROW_BLOCK, rows_per_seq)
    nb = nrows // rows
    n_piece = N_SET * (PCOLS // LANES)
    return pl.pallas_call(
        functools.partial(_s5_states_body, blocks_per_seq=rows_per_seq // rows, rows=rows),
        grid=(nb,),
        in_specs=[pl.BlockSpec((N_PART, rows, PW), lambda i: (0, i, 0)),
                  pl.BlockSpec((N_PART, rows, PW), lambda i: (0, nb - 1 - i, 0)),
                  _pick((2, N_PART, PW, PCOLS)), _pick((2, N_SET, SUBLANES, PCOLS))],
        out_specs=[pl.BlockSpec((N_PART, rows, PCOLS), lambda i: (0, i, 0)),
                   pl.BlockSpec((N_PART, rows, PCOLS), lambda i: (0, nb - 1 - i, 0))],
        out_shape=[jax.ShapeDtypeStruct((N_PART, nrows, PCOLS), BF16)] * 2,
        scratch_shapes=[pltpu.VMEM((rows * SUBLANES, LANES), F32)] * (2 * n_piece)
                       + [pltpu.VMEM((2 * n_piece, SUBLANES, LANES), F32)],
        compiler_params=_params(("arbitrary",)),
    )(hs, hs, wst, a)


def _s5_out_body(x_ref, sf_ref, sb_ref, m_ref, wout_ref, o_ref, *, rows):
    for c in range(N_SLAB):
        ys = []
        for k in range(PARTS_PER_SLAB):
            q = PARTS_PER_SLAB * c + k
            y = jnp.dot(x_ref[q], m_ref[q], preferred_element_type=F32)
            y += jnp.dot(sf_ref[q], wout_ref[0, q], preferred_element_type=F32)
            y += jnp.dot(sb_ref[q], wout_ref[1, q], preferred_element_type=F32)
            ys.append(y)
        for u in range(CHUNK // PARTS_PER_SLAB):
            tok = _parts_to_tok([y[:, u * LANES:(u + 1) * LANES] for y in ys])
            for m, piece in enumerate(tok):
                o_ref[c, pl.ds(PARTS_PER_SLAB * u + m, rows, stride=CHUNK), :] = piece


def _s5_out(hs, sf, sb, m, wout, rows_per_seq):
    nrows = hs.shape[1]
    rows = min(ROW_BLOCK, rows_per_seq)
    blk = lambda i: (0, i, 0)
    return pl.pallas_call(
        functools.partial(_s5_out_body, rows=rows),
        grid=(nrows // rows,),
        in_specs=[pl.BlockSpec((N_PART, rows, PW), blk), pl.BlockSpec((N_PART, rows, PCOLS), blk),
                  pl.BlockSpec((N_PART, rows, PCOLS), blk),
                  _pick((N_PART, PW, PW)), _pick((2, N_PART, PCOLS, PW))],
        out_specs=pl.BlockSpec((N_SLAB, rows * CHUNK, SLAB), blk),
        out_shape=jax.ShapeDtypeStruct((N_SLAB, nrows * CHUNK, SLAB), F32),
        compiler_params=_params(("parallel",)),
    )(hs, sf, sb, m, wout)


def _trunk(x, seq_len, norm_g, final_norm_g, w16, s5, s5_wa, s5_wb, pool_w, pool_scale, first=None):
    depth = norm_g.shape[0]
    gain = lambda layer, k: _pick((1, D_MODEL), layer, k)

    def ffn(x, g_spec, idx, **kw):
        nonlocal w16
        if w16 is not None:
            return _ffn(x, g_spec, idx, norm_g, *w16, seq_len=seq_len, **kw)
        own, stacks = first
        out = _ffn(x, g_spec, (), norm_g, *own, seq_len=seq_len, cast=stacks, **kw)
        shapes = ((depth, 2, D_MODEL, D_FF),) * 2 + ((depth, 2, D_FF, D_MODEL),)
        w16 = tuple(w.reshape(shp) for w, shp in zip(out[-3:], shapes))
        return out[:-3]

    for layer in range(depth):
        j = layer // 2
        last = dict(post="final", g2=final_norm_g, g2_spec=_pick((1, D_MODEL))) if layer == depth - 1 else {}
        if layer % 2 == 0:
            wst, wout, m, a = s5[j]
            x, hs = ffn(x, gain(layer, 0), (layer, 0), post="rows", g2=norm_g, g2_spec=gain(layer, 1))
            sf, sb = _s5_states(hs, wst, a, seq_len // CHUNK)
            y = _s5_out(hs, sf, sb, m, wout, seq_len // CHUNK)
            x = ffn(x, gain(layer, 2), (layer, 1), pre="glu", pre_args=(y, s5_wa, s5_wb, j), **last)[0]
        else:
            x = ffn(x, gain(layer, 0), (layer, 0))[0]
            x = ffn(x, gain(layer, 2), (layer, 1), pre="pool", pre_args=(gain(layer, 1), pool_w, pool_scale, j), **last)[0]
    return x, w16


def kernel(x_prompt, x_sample, norm_g, final_norm_g, ffn_w_gate, ffn_w_up, ffn_w_down, s5_lambda_re, s5_lambda_im, s5_log_step, s5_b_re, s5_b_im, s5_c_re, s5_c_im, s5_d, s5_w_glu_a, s5_w_glu_b, pool_w, pool_scale):
    stacks = (ffn_w_gate, ffn_w_up, ffn_w_down)
    first = (tuple(w[0, 0].astype(BF16) for w in stacks), tuple(w.reshape(-1, w.shape[-1]) for w in stacks))
    wa, wb, pw = s5_w_glu_a.astype(BF16), s5_w_glu_b.astype(BF16), pool_w.astype(BF16)
    norm_g = norm_g.reshape(norm_g.shape[0], 3, 1, D_MODEL)
    final_norm_g = final_norm_g.reshape(1, D_MODEL)
    pool_scale = pool_scale.reshape(-1, 1, D_MODEL)
    s5 = [_s5_prep(s5_lambda_re[j], s5_lambda_im[j], s5_log_step[j], s5_b_re[j], s5_b_im[j], s5_c_re[j], s5_c_im[j],
                   s5_d[j]) for j in range(s5_lambda_re.shape[0])]
    outs, w16 = [], None
    for x in (x_prompt, x_sample):
        bsz, seq, _ = x.shape
        y, w16 = _trunk(x.reshape(bsz * seq, D_MODEL), seq, norm_g, final_norm_g, w16, s5, wa, wb, pw, pool_scale, first)
        outs.append(y.reshape(bsz, seq, D_MODEL))
    return tuple(outs)
```

```python
import functools

import jax
import jax.numpy as jnp
from jax import lax
from jax.experimental import pallas as pl
from jax.experimental.pallas import tpu as pltpu

D_MODEL = 1024
D_FF = 2816
RMS_EPS = 1e-6
S5_GROUP = 16
S5_STATE = 64
POOL_WINDOWS = (2, 4, 8, 16)
POOL_CH = D_MODEL // len(POOL_WINDOWS)
POOL_HALO = 8

LANES = 128
SUBLANES = 8
BF16_ROWS = 16
MXU_COLS = 256
SLAB = LANES
N_SLAB = D_MODEL // SLAB
PART = 32
PARTS_PER_SLAB = SLAB // PART
N_PART = D_MODEL // PART
PSTATE = (PART // S5_GROUP) * S5_STATE
PCOLS = 2 * PSTATE
CHUNK = SUBLANES
PW = CHUNK * PART
N_SET = N_PART // SUBLANES
ROW_BLOCK = 128
TOKEN_TILE = 512
VMEM_LIMIT = 56 * 1024 * 1024

F32 = jnp.float32
BF16 = jnp.bfloat16


def _params(sem, vmem=VMEM_LIMIT):
    return pltpu.CompilerParams(dimension_semantics=sem, vmem_limit_bytes=vmem)


def _pick(tail, *lead):
    tail, lead = tuple(tail), tuple(lead)
    return pl.BlockSpec((None,) * len(lead) + tail, lambda *_: lead + (0,) * len(tail), pipeline_mode=pl.Buffered(1))


def _rms(x, g):
    return x * lax.rsqrt(jnp.mean(x * x, axis=-1, keepdims=True) + RMS_EPS) * g


def _tok_pos(t, k):
    return PARTS_PER_SLAB * (t // PARTS_PER_SLAB) + (t % PARTS_PER_SLAB + k) % PARTS_PER_SLAB


def _pos_tok(p, k):
    return PARTS_PER_SLAB * (p // PARTS_PER_SLAB) + (p % PARTS_PER_SLAB - k) % PARTS_PER_SLAB


def _pick_blocks(src, first):
    n = PARTS_PER_SLAB
    blk = lax.broadcasted_iota(jnp.int32, src[0].shape, 1) // PART
    out = src[(first + n - 1) % n]
    for p in range(n - 2, -1, -1):
        out = jnp.where(blk == p, src[(first + p) % n], out)
    return out


def _tok_to_parts(tok):
    n = PARTS_PER_SLAB
    rolled = [tok[0]] + [pltpu.roll(tok[m], m * PART, 1) for m in range(1, n)]
    return [_pick_blocks(rolled, -k) for k in range(n)]


def _parts_to_tok(parts):
    n = PARTS_PER_SLAB
    tok = []
    for m in range(n):
        z = _pick_blocks(parts, -m)
        tok.append(z if m == 0 else pltpu.roll(z, (n - m) * PART, 1))
    return tok


def _pool_pieces(x_ref, prev_ref, next_ref, g, w_ref, scale_ref, e_scr, inv_scr, o_ref, j, tiles_per_seq, seq_len):
    lo, n = 2 * POOL_HALO, TOKEN_TILE + 4 * POOL_HALO
    pad = jnp.zeros((POOL_HALO, D_MODEL), F32)
    e_scr[...] = jnp.concatenate([pad, jnp.where(j == 0, 0.0, _rms(prev_ref[...], g)), _rms(x_ref[...], g),
                                  jnp.where(j == tiles_per_seq - 1, 0.0, _rms(next_ref[...], g)), pad], axis=0)
    pos = j * TOKEN_TILE + lax.broadcasted_iota(jnp.int32, (TOKEN_TILE, 1), 0)
    for gi, win in enumerate(POOL_WINDOWS):
        cnt = jnp.minimum(pos + (win - win // 2), seq_len) - jnp.maximum(pos - win // 2, 0)
        inv_scr[gi] = jnp.broadcast_to(1.0 / cnt.astype(F32), (TOKEN_TILE, LANES))
    yield
    for gi, win in enumerate(POOL_WINDOWS):
        ps = []
        for half in range(POOL_CH // LANES):
            cols = slice(gi * POOL_CH + half * LANES, gi * POOL_CH + (half + 1) * LANES)
            c = e_scr[:, cols]
            w = 1
            while w < win:
                c = c + pltpu.roll(c, w, 0)
                w *= 2
            lead = win - win // 2 - 1
            if lead:
                c = pltpu.roll(c, n - lead, 0)
            ps.append((c[lo:lo + TOKEN_TILE] * inv_scr[gi] - e_scr[lo:lo + TOKEN_TILE, cols]).astype(BF16))
            yield
        cols = slice(gi * POOL_CH, (gi + 1) * POOL_CH)
        z = jnp.dot(jnp.concatenate(ps, axis=1), w_ref[gi], preferred_element_type=F32)
        o_ref[:, cols] = x_ref[:, cols] + z * scale_ref[:, cols]
        yield


def _ffn_core(x, g_ref, wg_ref, wu_ref, wd_ref, between=None):
    h = _rms(x, g_ref[...]).astype(BF16)
    gate = jnp.dot(h, wg_ref[...], preferred_element_type=F32)
    if between is not None:
        between()
    up = jnp.dot(h, wu_ref[...], preferred_element_type=F32)
    act = (gate * jax.nn.sigmoid(gate) * up).astype(BF16)
    return x + 0.5 * jnp.dot(act, wd_ref[...], preferred_element_type=F32)


def _ffn_core_with(side, x_ref, g_ref, wg_ref, wu_ref, wd_ref):
    h = _rms(x_ref[...], g_ref[...]).astype(BF16)
    acts = []
    for c in range(D_FF // MXU_COLS):
        cols = slice(c * MXU_COLS, (c + 1) * MXU_COLS)
        gate = jnp.dot(h, wg_ref[:, cols], preferred_element_type=F32)
        up = jnp.dot(h, wu_ref[:, cols], preferred_element_type=F32)
        acts.append((gate * jax.nn.sigmoid(gate) * up).astype(BF16))
        next(side, None)
    act = jnp.concatenate(acts, axis=1)
    ys = []
    for c in range(D_MODEL // MXU_COLS):
        cols = slice(c * MXU_COLS, (c + 1) * MXU_COLS)
        ys.append(x_ref[:, cols] + 0.5 * jnp.dot(act, wd_ref[:, cols], preferred_element_type=F32))
        next(side, None)
    for _ in side:
        pass
    return jnp.concatenate(ys, axis=1)


def _rows_pieces(h_scr, hs_ref):
    rows = TOKEN_TILE // CHUNK
    for c in range(N_SLAB):
        for u in range(CHUNK // PARTS_PER_SLAB):
            tok = [h_scr[c, pl.ds(PARTS_PER_SLAB * u + m, rows, stride=CHUNK), :] for m in range(PARTS_PER_SLAB)]
            for k, piece in enumerate(_tok_to_parts(tok)):
                hs_ref[PARTS_PER_SLAB * c + k, :, u * LANES:(u + 1) * LANES] = piece.astype(BF16)
        yield


def _by_parity(i, first, stage, buf0, buf1):
    @pl.when(i == 0)
    def _():
        first(buf0)

    @pl.when((i > 0) & (i % 2 == 0))
    def _():
        stage(buf0, buf1)

    @pl.when(i % 2 == 1)
    def _():
        stage(buf1, buf0)


def _ffn_body(*refs, pre, post, n_cast, n_tiles, tiles_per_seq, seq_len):
    it = iter(refs)
    x_ref = next(it)
    if pre == "glu":
        y_ref, wa_ref, wb_ref = next(it), next(it), next(it)
    if pre == "pool":
        prev_ref, next_ref, pg_ref, pw_ref, ps_ref = next(it), next(it), next(it), next(it), next(it)
    g_ref, wg_ref, wu_ref, wd_ref = next(it), next(it), next(it), next(it)
    if post is not None:
        g2_ref = next(it)
    cast_in = [next(it) for _ in range(n_cast)]
    o_ref = next(it)
    if post == "rows":
        hs_ref = next(it)
    cast_out = [next(it) for _ in range(n_cast)]
    if pre == "pool":
        buf0, buf1, e_scr, inv_scr = next(it), next(it), next(it), next(it)
    if post == "rows":
        h_scr = next(it)
    i = pl.program_id(0)

    def cast_slabs():
        for src, dst in zip(cast_in, cast_out):
            dst[...] = src[...].astype(BF16)

    ffn = functools.partial(_ffn_core, g_ref=g_ref, wg_ref=wg_ref, wu_ref=wu_ref, wd_ref=wd_ref,
                            between=cast_slabs if n_cast else None)

    if pre == "pool":
        def pool(mixed_w):
            j = jnp.minimum(i, n_tiles - 1) % tiles_per_seq
            return _pool_pieces(x_ref, prev_ref, next_ref, pg_ref[...], pw_ref, ps_ref, e_scr, inv_scr, mixed_w,
                                j, tiles_per_seq, seq_len)

        def first(mixed_w):
            for _ in pool(mixed_w):
                pass

        def stage(mixed_w, mixed_r):
            y = _ffn_core_with(pool(mixed_w), mixed_r, g_ref, wg_ref, wu_ref, wd_ref)
            o_ref[...] = _rms(y, g2_ref[...]) if post == "final" else y
        _by_parity(i, first, stage, buf0, buf1)
        return

    def x_in():
        x = x_ref[...]
        if pre == "glu":
            y = jnp.concatenate([y_ref[c] for c in range(N_SLAB)], axis=1)
            gy = jax.nn.gelu(y).astype(BF16)
            a = jnp.dot(gy, wa_ref[...], preferred_element_type=F32)
            b = jnp.dot(gy, wb_ref[...], preferred_element_type=F32)
            x = x + a * jax.nn.sigmoid(b)
        return x

    y = ffn(x_in())
    o_ref[...] = _rms(y, g2_ref[...]) if post == "final" else y
    if post == "rows":
        h2 = _rms(y, g2_ref[...])
        for c in range(N_SLAB):
            h_scr[c] = h2[:, c * SLAB:(c + 1) * SLAB]
        for _ in _rows_pieces(h_scr, hs_ref):
            pass


def _ffn(x, g_spec, w_idx, norm_g, wg, wu, wd, *, seq_len, pre=None, pre_args=(), post=None, g2=None, g2_spec=None,
         cast=()):
    assert not (pre == "pool" and (post == "rows" or cast)), "the skewed pool stage has no extra outputs"
    n = x.shape[0]
    n_tiles = n // TOKEN_TILE
    skewed = pre == "pool"
    cur = (lambda i: jnp.minimum(i, n_tiles - 1)) if skewed else (lambda i: i)
    tile_in = pl.BlockSpec((TOKEN_TILE, D_MODEL), lambda i: (cur(i), 0))
    tile_out = pl.BlockSpec((TOKEN_TILE, D_MODEL), (lambda i: (jnp.maximum(i - 1, 0), 0)) if skewed else (lambda i: (i, 0)))
    args, specs, scratch = [x], [tile_in], []
    if pre == "glu":
        y, wa, wb, j = pre_args
        args += [y, wa, wb]
        specs += [pl.BlockSpec((N_SLAB, TOKEN_TILE, SLAB), lambda i: (0, cur(i), 0)),
                  _pick((D_MODEL, D_MODEL), j), _pick((D_MODEL, D_MODEL), j)]
    if pre == "pool":
        pg_spec, pw, ps, j = pre_args
        halo_blocks = TOKEN_TILE // POOL_HALO
        last = n // POOL_HALO - 1
        args += [x, x, norm_g, pw, ps]
        specs += [pl.BlockSpec((POOL_HALO, D_MODEL), lambda i: (jnp.maximum(cur(i) * halo_blocks - 1, 0), 0)),
                  pl.BlockSpec((POOL_HALO, D_MODEL), lambda i: (jnp.minimum((cur(i) + 1) * halo_blocks, last), 0)),
                  pg_spec, _pick((len(POOL_WINDOWS), POOL_CH, POOL_CH), j), _pick((1, D_MODEL), j)]
        scratch += [pltpu.VMEM((TOKEN_TILE, D_MODEL), F32)] * 2
        scratch += [pltpu.VMEM((TOKEN_TILE + 4 * POOL_HALO, D_MODEL), F32),
                    pltpu.VMEM((len(POOL_WINDOWS), TOKEN_TILE, LANES), F32)]
    args += [norm_g, wg, wu, wd]
    specs += [g_spec, _pick((D_MODEL, D_FF), *w_idx), _pick((D_MODEL, D_FF), *w_idx), _pick((D_FF, D_MODEL), *w_idx)]
    if post is not None:
        args.append(g2)
        specs.append(g2_spec)
    out_shape, out_specs = [jax.ShapeDtypeStruct((n, D_MODEL), F32)], [tile_out]
    if post == "rows":
        rows = TOKEN_TILE // CHUNK
        out_shape.append(jax.ShapeDtypeStruct((N_PART, n // CHUNK, PW), BF16))
        out_specs.append(pl.BlockSpec((N_PART, rows, PW), lambda i: (0, i, 0)))
        scratch.append(pltpu.VMEM((N_SLAB, TOKEN_TILE, SLAB), F32))
    for w in cast:
        slab = pl.BlockSpec((w.shape[0] // n_tiles, w.shape[1]), lambda i: (i, 0))
        assert w.shape[0] % (n_tiles * BF16_ROWS) == 0
        args.append(w)
        specs.append(slab)
        out_shape.append(jax.ShapeDtypeStruct(w.shape, BF16))
        out_specs.append(slab)
    return pl.pallas_call(
        functools.partial(_ffn_body, pre=pre, post=post, n_cast=len(cast), n_tiles=n_tiles,
                          tiles_per_seq=seq_len // TOKEN_TILE, seq_len=seq_len),
        grid=(n_tiles + 1 if skewed else n_tiles,),
        in_specs=specs, out_specs=out_specs, out_shape=out_shape, scratch_shapes=scratch,
        compiler_params=_params(("arbitrary",) if skewed else ("parallel",)),
    )(*args)


def _cmul(ar, ai, br, bi):
    return ar * br - ai * bi, ar * bi + ai * br


def _dot_nt(a, b):
    return lax.dot_general(a, b, (((1,), (1,)), ((), ())), precision=lax.Precision.HIGHEST, preferred_element_type=F32)


def _s5_prep_part(k, lr_ref, li_ref, ls_ref, br_ref, bi_ref, cr_ref, ci_ref, d_ref, wst_ref, wout_ref, m_ref, a_ref):
    bs, cs = [], []
    for d in range(2):
        lr, li = lr_ref[k, d], li_ref[k, d]
        step = jnp.exp(ls_ref[k, d])
        mag = jnp.exp(lr * step)
        ar, ai = mag * jnp.cos(li * step), mag * jnp.sin(li * step)
        den = lr * lr + li * li
        nr = ar - 1.0
        kr, ki = (nr * lr + ai * li) / den, (ai * lr - nr * li) / den
        bbr, bbi = _cmul(kr, ki, br_ref[k, d], bi_ref[k, d])
        cre, cim = cr_ref[k, d], ci_ref[k, d]
        pr, pi = jnp.ones_like(ar), jnp.zeros_like(ar)
        bs.append([])
        cs.append([])
        for j in range(CHUNK + 1):
            xr, xi = _cmul(pr, pi, bbr, bbi)
            bs[d].append(jnp.concatenate([xr, xi], axis=1))
            yr, yi = _cmul(pr, pi, cre, cim)
            cs[d].append(jnp.concatenate([yr, -yi], axis=1))
            if j == CHUNK:
                a_ref[d, k] = jnp.concatenate([pr, pi], axis=1)
            pr, pi = _cmul(pr, pi, ar, ai)
        toks = [_pos_tok(p, k) for p in range(CHUNK)]
        for p, s in enumerate(toks):
            j = CHUNK - 1 - s if d == 0 else s
            wst_ref[d, k, p * PART:(p + 1) * PART, :] = bs[d][j].astype(BF16)
        wout_t = jnp.concatenate([cs[d][t + 1 if d == 0 else CHUNK - t] for t in toks], axis=0)
        wout_ref[d, k] = wout_t.T.astype(BF16)
    kf = _dot_nt(bs[0][0], jnp.concatenate([cs[0][j] for j in range(CHUNK)], axis=0))
    kb = _dot_nt(bs[1][0], jnp.concatenate([cs[1][CHUNK - 1 - j] for j in range(CHUNK)], axis=0))
    ri = lax.broadcasted_iota(jnp.int32, (PART, PW), 0)
    ci = lax.broadcasted_iota(jnp.int32, (PART, PW), 1)
    for s in range(CHUNK):
        f = kf if s == 0 else pltpu.roll(kf, s * PART, 1)
        b = kb if s == CHUNK - 1 else pltpu.roll(kb, (s + 1) * PART, 1)
        blk = (jnp.where(ci >= s * PART, f, 0.0) + jnp.where(ci < (s + 1) * PART, b, 0.0)
               + jnp.where(ci == ri + s * PART, d_ref[k], 0.0))
        if k:
            blk = jnp.concatenate([pltpu.roll(blk[:, u * LANES:(u + 1) * LANES], k * PART, 1)
                                   for u in range(PW // LANES)], axis=1)
        p = _tok_pos(s, k)
        m_ref[k, p * PART:(p + 1) * PART, :] = blk.astype(BF16)


def _s5_prep_body(*refs):
    for k in range(PARTS_PER_SLAB):
        _s5_prep_part(k, *refs)


def _s5_prep(lam_re, lam_im, log_step, b_re, b_im, c_re, c_im, d):
    gpp = PART // S5_GROUP
    eye = jnp.eye(gpp, dtype=F32)

    def state_rows(v):
        return v.reshape(2, N_PART, 1, PSTATE).transpose(1, 0, 2, 3)

    def block_diag(v):
        v = v.transpose(1, 0, 2, 3, 4)
        v = v[:, :, :, :, None, :] * eye[None, None, :, None, :, None]
        return v.reshape(N_PART, 2, PART, PSTATE)

    ls = jnp.broadcast_to(log_step[:, :, None], lam_re.shape)
    bshape = (2, N_PART, gpp, S5_STATE, S5_GROUP)
    cshape = (2, N_PART, gpp, S5_GROUP, S5_STATE)
    args = (state_rows(lam_re), state_rows(lam_im), state_rows(ls),
            block_diag(b_re.reshape(bshape).transpose(0, 1, 2, 4, 3)), block_diag(b_im.reshape(bshape).transpose(0, 1, 2, 4, 3)),
            block_diag(c_re.reshape(cshape)), block_diag(c_im.reshape(cshape)),
            jnp.tile(d.reshape(N_PART, 1, PART), (1, 1, CHUNK)))
    pps = PARTS_PER_SLAB
    vec = pl.BlockSpec((pps, 2, 1, PSTATE), lambda c: (c, 0, 0, 0))
    mat = pl.BlockSpec((pps, 2, PART, PSTATE), lambda c: (c, 0, 0, 0))
    wst, wout, m, a = pl.pallas_call(
        _s5_prep_body,
        grid=(N_SLAB,),
        in_specs=[vec, vec, vec, mat, mat, mat, mat, pl.BlockSpec((pps, 1, PW), lambda c: (c, 0, 0))],
        out_specs=[pl.BlockSpec((2, pps, PW, PCOLS), lambda c: (0, c, 0, 0)),
                   pl.BlockSpec((2, pps, PCOLS, PW), lambda c: (0, c, 0, 0)),
                   pl.BlockSpec((pps, PW, PW), lambda c: (c, 0, 0)),
                   pl.BlockSpec((2, pps, 1, PCOLS), lambda c: (0, c, 0, 0))],
        out_shape=[jax.ShapeDtypeStruct((2, N_PART, PW, PCOLS), BF16),
                   jax.ShapeDtypeStruct((2, N_PART, PCOLS, PW), BF16),
                   jax.ShapeDtypeStruct((N_PART, PW, PW), BF16),
                   jax.ShapeDtypeStruct((2, N_PART, 1, PCOLS), F32)],
        compiler_params=_params(("parallel",)),
    )(*args)
    return wst, wout, m, a.reshape(2, N_SET, SUBLANES, PCOLS)


def _s5_states_body(xf_ref, xb_ref, wst_ref, a_ref, of_ref, ob_ref, *scr, blocks_per_seq, rows):
    x_refs, o_refs = (xf_ref, xb_ref), (of_ref, ob_ref)
    halves = PCOLS // LANES
    n_piece = N_SET * halves
    car_scr = scr[2 * n_piece]
    piece = lambda d, st, ri: scr[d * n_piece + halves * st + ri]

    @pl.when(pl.program_id(0) % blocks_per_seq == 0)
    def _():
        car_scr[...] = jnp.zeros_like(car_scr)

    for d in range(2):
        for q in range(N_PART):
            s_loc = jnp.dot(x_refs[d][q], wst_ref[d, q], preferred_element_type=F32)
            for ri in range(halves):
                piece(d, q // SUBLANES, ri)[pl.ds(q % SUBLANES, rows, stride=SUBLANES), :] = (
                    s_loc[:, ri * LANES:(ri + 1) * LANES])

    def step(n, carry):
        new = []
        for d in range(2):
            r = n if d == 0 else rows - 1 - n
            r0 = pl.multiple_of(r * SUBLANES, SUBLANES)
            for st in range(N_SET):
                sr, si = carry[(d * N_SET + st) * 2], carry[(d * N_SET + st) * 2 + 1]
                ar, ai = a_ref[d, st, :, :LANES], a_ref[d, st, :, LANES:]
                re_scr, im_scr = piece(d, st, 0), piece(d, st, 1)
                xr, xi = re_scr[pl.ds(r0, SUBLANES), :], im_scr[pl.ds(r0, SUBLANES), :]
                re_scr[pl.ds(r0, SUBLANES), :] = sr
                im_scr[pl.ds(r0, SUBLANES), :] = si
                new += [ar * sr - ai * si + xr, ar * si + ai * sr + xi]
        return tuple(new)

    n_car = 2 * n_piece
    carry = lax.fori_loop(0, rows, step, tuple(car_scr[k] for k in range(n_car)), unroll=4)
    for k in range(n_car):
        car_scr[k] = carry[k]
    for d in range(2):
        for q in range(N_PART):
            for ri in range(halves):
                o_refs[d][q, :, ri * LANES:(ri + 1) * LANES] = (
                    piece(d, q // SUBLANES, ri)[pl.ds(q % SUBLANES, rows, stride=SUBLANES), :].astype(BF16))


def _s5_states(hs, wst, a, rows_per_seq):
    nrows = hs.shape[1]
    rows = min(ROW_BLOCK, rows_per_seq)
    nb = nrows // rows
    n_piece = N_SET * (PCOLS // LANES)
    return pl.pallas_call(
        functools.partial(_s5_states_body, blocks_per_seq=rows_per_seq // rows, rows=rows),
        grid=(nb,),
        in_specs=[pl.BlockSpec((N_PART, rows, PW), lambda i: (0, i, 0)),
                  pl.BlockSpec((N_PART, rows, PW), lambda i: (0, nb - 1 - i, 0)),
                  _pick((2, N_PART, PW, PCOLS)), _pick((2, N_SET, SUBLANES, PCOLS))],
        out_specs=[pl.BlockSpec((N_PART, rows, PCOLS), lambda i: (0, i, 0)),
                   pl.BlockSpec((N_PART, rows, PCOLS), lambda i: (0, nb - 1 - i, 0))],
        out_shape=[jax.ShapeDtypeStruct((N_PART, nrows, PCOLS), BF16)] * 2,
        scratch_shapes=[pltpu.VMEM((rows * SUBLANES, LANES), F32)] * (2 * n_piece)
                       + [pltpu.VMEM((2 * n_piece, SUBLANES, LANES), F32)],
        compiler_params=_params(("arbitrary",)),
    )(hs, hs, wst, a)


def _s5_out_body(x_ref, sf_ref, sb_ref, m_ref, wout_ref, o_ref, *, rows):
    for c in range(N_SLAB):
        ys = []
        for k in range(PARTS_PER_SLAB):
            q = PARTS_PER_SLAB * c + k
            y = jnp.dot(x_ref[q], m_ref[q], preferred_element_type=F32)
            y += jnp.dot(sf_ref[q], wout_ref[0, q], preferred_element_type=F32)
            y += jnp.dot(sb_ref[q], wout_ref[1, q], preferred_element_type=F32)
            ys.append(y)
        for u in range(CHUNK // PARTS_PER_SLAB):
            tok = _parts_to_tok([y[:, u * LANES:(u + 1) * LANES] for y in ys])
            for m, piece in enumerate(tok):
                o_ref[c, pl.ds(PARTS_PER_SLAB * u + m, rows, stride=CHUNK), :] = piece


def _s5_out(hs, sf, sb, m, wout, rows_per_seq):
    nrows = hs.shape[1]
    rows = min(ROW_BLOCK, rows_per_seq)
    blk = lambda i: (0, i, 0)
    return pl.pallas_call(
        functools.partial(_s5_out_body, rows=rows),
        grid=(nrows // rows,),
        in_specs=[pl.BlockSpec((N_PART, rows, PW), blk), pl.BlockSpec((N_PART, rows, PCOLS), blk),
                  pl.BlockSpec((N_PART, rows, PCOLS), blk),
                  _pick((N_PART, PW, PW)), _pick((2, N_PART, PCOLS, PW))],
        out_specs=pl.BlockSpec((N_SLAB, rows * CHUNK, SLAB), blk),
        out_shape=jax.ShapeDtypeStruct((N_SLAB, nrows * CHUNK, SLAB), F32),
        compiler_params=_params(("parallel",)),
    )(hs, sf, sb, m, wout)


def _trunk(x, seq_len, norm_g, final_norm_g, w16, s5, pool_scale, first=None):
    depth = norm_g.shape[0]
    gain = lambda layer, k: _pick((1, D_MODEL), layer, k)

    def ffn(x, g_spec, idx, **kw):
        nonlocal w16
        if w16 is not None:
            return _ffn(x, g_spec, idx, norm_g, *w16[:3], seq_len=seq_len, **kw)
        own, stacks = first
        out = _ffn(x, g_spec, (), norm_g, *own, seq_len=seq_len,
                   cast=tuple(w.reshape(-1, w.shape[-1]) for w in stacks), **kw)
        w16 = tuple(w.reshape(f.shape) for w, f in zip(out[-len(stacks):], stacks))
        return out[:-len(stacks)]

    for layer in range(depth):
        j = layer // 2
        last = dict(post="final", g2=final_norm_g, g2_spec=_pick((1, D_MODEL))) if layer == depth - 1 else {}
        if layer % 2 == 0:
            wst, wout, m, a = s5[j]
            x, hs = ffn(x, gain(layer, 0), (layer, 0), post="rows", g2=norm_g, g2_spec=gain(layer, 1))
            sf, sb = _s5_states(hs, wst, a, seq_len // CHUNK)
            y = _s5_out(hs, sf, sb, m, wout, seq_len // CHUNK)
            x = ffn(x, gain(layer, 2), (layer, 1), pre="glu", pre_args=(y, w16[3], w16[4], j), **last)[0]
        else:
            x = ffn(x, gain(layer, 0), (layer, 0))[0]
            x = ffn(x, gain(layer, 2), (layer, 1), pre="pool", pre_args=(gain(layer, 1), w16[5], pool_scale, j), **last)[0]
    return x, w16


def kernel(x_prompt, x_sample, norm_g, final_norm_g, ffn_w_gate, ffn_w_up, ffn_w_down, s5_lambda_re, s5_lambda_im, s5_log_step, s5_b_re, s5_b_im, s5_c_re, s5_c_im, s5_d, s5_w_glu_a, s5_w_glu_b, pool_w, pool_scale):
    stacks = (ffn_w_gate, ffn_w_up, ffn_w_down, s5_w_glu_a, s5_w_glu_b, pool_w)
    first = (tuple(w[0, 0].astype(BF16) for w in stacks[:3]), stacks)
    norm_g = norm_g.reshape(norm_g.shape[0], 3, 1, D_MODEL)
    final_norm_g = final_norm_g.reshape(1, D_MODEL)
    pool_scale = pool_scale.reshape(-1, 1, D_MODEL)
    s5 = [_s5_prep(s5_lambda_re[j], s5_lambda_im[j], s5_log_step[j], s5_b_re[j], s5_b_im[j], s5_c_re[j], s5_c_im[j],
                   s5_d[j]) for j in range(s5_lambda_re.shape[0])]
    outs, w16 = [], None
    for x in (x_prompt, x_sample):
        bsz, seq, _ = x.shape
        y, w16 = _trunk(x.reshape(bsz * seq, D_MODEL), seq, norm_g, final_norm_g, w16, s5, pool_scale, first)
        outs.append(y.reshape(bsz, seq, D_MODEL))
    return tuple(outs)
```

```python
import functools

import jax
import jax.numpy as jnp
from jax import lax
from jax.experimental import pallas as pl
from jax.experimental.pallas import tpu as pltpu

D_MODEL = 1024
D_FF = 2816
RMS_EPS = 1e-6
S5_GROUP = 16
S5_STATE = 64
POOL_WINDOWS = (2, 4, 8, 16)
POOL_CH = D_MODEL // len(POOL_WINDOWS)
POOL_HALO = 8

LANES = 128
SUBLANES = 8
BF16_ROWS = 16
MXU_COLS = 256
SLAB = LANES
N_SLAB = D_MODEL // SLAB
PART = 32
PARTS_PER_SLAB = SLAB // PART
N_PART = D_MODEL // PART
PSTATE = (PART // S5_GROUP) * S5_STATE
PCOLS = 2 * PSTATE
CHUNK = SUBLANES
PW = CHUNK * PART
N_SET = N_PART // SUBLANES
ROW_BLOCK = 128
TOKEN_TILE = 512
VMEM_LIMIT = 56 * 1024 * 1024

F32 = jnp.float32
BF16 = jnp.bfloat16


def _params(sem, vmem=VMEM_LIMIT):
    return pltpu.CompilerParams(dimension_semantics=sem, vmem_limit_bytes=vmem)


def _pick(tail, *lead):
    tail, lead = tuple(tail), tuple(lead)
    return pl.BlockSpec((None,) * len(lead) + tail, lambda *_: lead + (0,) * len(tail), pipeline_mode=pl.Buffered(1))


def _rms(x, g):
    return x * lax.rsqrt(jnp.mean(x * x, axis=-1, keepdims=True) + RMS_EPS) * g


def _tok_pos(t, k):
    return PARTS_PER_SLAB * (t // PARTS_PER_SLAB) + (t % PARTS_PER_SLAB + k) % PARTS_PER_SLAB


def _pos_tok(p, k):
    return PARTS_PER_SLAB * (p // PARTS_PER_SLAB) + (p % PARTS_PER_SLAB - k) % PARTS_PER_SLAB


def _pick_blocks(src, first):
    n = PARTS_PER_SLAB
    blk = lax.broadcasted_iota(jnp.int32, src[0].shape, 1) // PART
    out = src[(first + n - 1) % n]
    for p in range(n - 2, -1, -1):
        out = jnp.where(blk == p, src[(first + p) % n], out)
    return out


def _tok_to_parts(tok):
    n = PARTS_PER_SLAB
    rolled = [tok[0]] + [pltpu.roll(tok[m], m * PART, 1) for m in range(1, n)]
    return [_pick_blocks(rolled, -k) for k in range(n)]


def _parts_to_tok(parts):
    n = PARTS_PER_SLAB
    tok = []
    for m in range(n):
        z = _pick_blocks(parts, -m)
        tok.append(z if m == 0 else pltpu.roll(z, (n - m) * PART, 1))
    return tok


def _pool_pieces(x_ref, prev_ref, next_ref, g, w_ref, scale_ref, e_scr, inv_scr, o_ref, j, tiles_per_seq, seq_len):
    lo, n = 2 * POOL_HALO, TOKEN_TILE + 4 * POOL_HALO
    pad = jnp.zeros((POOL_HALO, D_MODEL), F32)
    e_scr[...] = jnp.concatenate([pad, jnp.where(j == 0, 0.0, _rms(prev_ref[...], g)), _rms(x_ref[...], g),
                                  jnp.where(j == tiles_per_seq - 1, 0.0, _rms(next_ref[...], g)), pad], axis=0)
    pos = j * TOKEN_TILE + lax.broadcasted_iota(jnp.int32, (TOKEN_TILE, 1), 0)
    for gi, win in enumerate(POOL_WINDOWS):
        cnt = jnp.minimum(pos + (win - win // 2), seq_len) - jnp.maximum(pos - win // 2, 0)
        inv_scr[gi] = jnp.broadcast_to(1.0 / cnt.astype(F32), (TOKEN_TILE, LANES))
    yield
    for gi, win in enumerate(POOL_WINDOWS):
        ps = []
        for half in range(POOL_CH // LANES):
            cols = slice(gi * POOL_CH + half * LANES, gi * POOL_CH + (half + 1) * LANES)
            c = e_scr[:, cols]
            w = 1
            while w < win:
                c = c + pltpu.roll(c, w, 0)
                w *= 2
            lead = win - win // 2 - 1
            if lead:
                c = pltpu.roll(c, n - lead, 0)
            ps.append((c[lo:lo + TOKEN_TILE] * inv_scr[gi] - e_scr[lo:lo + TOKEN_TILE, cols]).astype(BF16))
            yield
        cols = slice(gi * POOL_CH, (gi + 1) * POOL_CH)
        z = jnp.dot(jnp.concatenate(ps, axis=1), w_ref[gi], preferred_element_type=F32)
        o_ref[:, cols] = x_ref[:, cols] + z * scale_ref[:, cols]
        yield


def _ffn_core(x, g_ref, wg_ref, wu_ref, wd_ref, between=None):
    h = _rms(x, g_ref[...]).astype(BF16)
    gate = jnp.dot(h, wg_ref[...], preferred_element_type=F32)
    if between is not None:
        between()
    up = jnp.dot(h, wu_ref[...], preferred_element_type=F32)
    act = (gate * jax.nn.sigmoid(gate) * up).astype(BF16)
    return x + 0.5 * jnp.dot(act, wd_ref[...], preferred_element_type=F32)


def _ffn_core_with(side, x_ref, g_ref, wg_ref, wu_ref, wd_ref):
    h = _rms(x_ref[...], g_ref[...]).astype(BF16)
    acts = []
    for c in range(D_FF // MXU_COLS):
        cols = slice(c * MXU_COLS, (c + 1) * MXU_COLS)
        gate = jnp.dot(h, wg_ref[:, cols], preferred_element_type=F32)
        up = jnp.dot(h, wu_ref[:, cols], preferred_element_type=F32)
        acts.append((gate * jax.nn.sigmoid(gate) * up).astype(BF16))
        next(side, None)
    act = jnp.concatenate(acts, axis=1)
    ys = []
    for c in range(D_MODEL // MXU_COLS):
        cols = slice(c * MXU_COLS, (c + 1) * MXU_COLS)
        ys.append(x_ref[:, cols] + 0.5 * jnp.dot(act, wd_ref[:, cols], preferred_element_type=F32))
        next(side, None)
    for _ in side:
        pass
    return jnp.concatenate(ys, axis=1)


def _rows_pieces(h_scr, hs_ref):
    rows = TOKEN_TILE // CHUNK
    for c in range(N_SLAB):
        for u in range(CHUNK // PARTS_PER_SLAB):
            tok = [h_scr[c, pl.ds(PARTS_PER_SLAB * u + m, rows, stride=CHUNK), :] for m in range(PARTS_PER_SLAB)]
            for k, piece in enumerate(_tok_to_parts(tok)):
                hs_ref[PARTS_PER_SLAB * c + k, :, u * LANES:(u + 1) * LANES] = piece.astype(BF16)
        yield


def _by_parity(i, first, stage, buf0, buf1):
    @pl.when(i == 0)
    def _():
        first(buf0)

    @pl.when((i > 0) & (i % 2 == 0))
    def _():
        stage(buf0, buf1)

    @pl.when(i % 2 == 1)
    def _():
        stage(buf1, buf0)


def _ffn_body(*refs, pre, post, n_cast, n_tiles, tiles_per_seq, seq_len):
    it = iter(refs)
    x_ref = next(it)
    if pre == "glu":
        y_ref, wa_ref, wb_ref = next(it), next(it), next(it)
    if pre == "pool":
        prev_ref, next_ref, pg_ref, pw_ref, ps_ref = next(it), next(it), next(it), next(it), next(it)
    g_ref, wg_ref, wu_ref, wd_ref = next(it), next(it), next(it), next(it)
    if post is not None:
        g2_ref = next(it)
    cast_in = [next(it) for _ in range(n_cast)]
    o_ref = next(it)
    if post == "rows":
        hs_ref = next(it)
    cast_out = [next(it) for _ in range(n_cast)]
    if pre == "pool":
        buf0, buf1, e_scr, inv_scr = next(it), next(it), next(it), next(it)
    if post == "rows":
        h_scr = next(it)
    i = pl.program_id(0)

    def cast_slabs():
        for src, dst in zip(cast_in, cast_out):
            dst[...] = src[...].astype(BF16)

    ffn = functools.partial(_ffn_core, g_ref=g_ref, wg_ref=wg_ref, wu_ref=wu_ref, wd_ref=wd_ref,
                            between=cast_slabs if n_cast else None)

    if pre == "pool":
        def pool(mixed_w):
            j = jnp.minimum(i, n_tiles - 1) % tiles_per_seq
            return _pool_pieces(x_ref, prev_ref, next_ref, pg_ref[...], pw_ref, ps_ref, e_scr, inv_scr, mixed_w,
                                j, tiles_per_seq, seq_len)

        def first(mixed_w):
            for _ in pool(mixed_w):
                pass

        def stage(mixed_w, mixed_r):
            y = _ffn_core_with(pool(mixed_w), mixed_r, g_ref, wg_ref, wu_ref, wd_ref)
            o_ref[...] = _rms(y, g2_ref[...]) if post == "final" else y
        _by_parity(i, first, stage, buf0, buf1)
        return

    def x_in():
        x = x_ref[...]
        if pre == "glu":
            y = jnp.concatenate([y_ref[c] for c in range(N_SLAB)], axis=1)
            gy = jax.nn.gelu(y).astype(BF16)
            a = jnp.dot(gy, wa_ref[...], preferred_element_type=F32)
            b = jnp.dot(gy, wb_ref[...], preferred_element_type=F32)
            x = x + a * jax.nn.sigmoid(b)
        return x

    y = ffn(x_in())
    o_ref[...] = _rms(y, g2_ref[...]) if post == "final" else y
    if post == "rows":
        h2 = _rms(y, g2_ref[...])
        for c in range(N_SLAB):
            h_scr[c] = h2[:, c * SLAB:(c + 1) * SLAB]
        for _ in _rows_pieces(h_scr, hs_ref):
            pass


def _ffn(x, g_spec, w_idx, norm_g, wg, wu, wd, *, seq_len, pre=None, pre_args=(), post=None, g2=None, g2_spec=None,
         cast=()):
    assert not (pre == "pool" and (post == "rows" or cast)), "the skewed pool stage has no extra outputs"
    n = x.shape[0]
    n_tiles = n // TOKEN_TILE
    skewed = pre == "pool"
    cur = (lambda i: jnp.minimum(i, n_tiles - 1)) if skewed else (lambda i: i)
    tile_in = pl.BlockSpec((TOKEN_TILE, D_MODEL), lambda i: (cur(i), 0))
    tile_out = pl.BlockSpec((TOKEN_TILE, D_MODEL), (lambda i: (jnp.maximum(i - 1, 0), 0)) if skewed else (lambda i: (i, 0)))
    args, specs, scratch = [x], [tile_in], []
    if pre == "glu":
        y, wa, wb, j = pre_args
        args += [y, wa, wb]
        specs += [pl.BlockSpec((N_SLAB, TOKEN_TILE, SLAB), lambda i: (0, cur(i), 0)),
                  _pick((D_MODEL, D_MODEL), j), _pick((D_MODEL, D_MODEL), j)]
    if pre == "pool":
        pg_spec, pw, ps, j = pre_args
        halo_blocks = TOKEN_TILE // POOL_HALO
        last = n // POOL_HALO - 1
        args += [x, x, norm_g, pw, ps]
        specs += [pl.BlockSpec((POOL_HALO, D_MODEL), lambda i: (jnp.maximum(cur(i) * halo_blocks - 1, 0), 0)),
                  pl.BlockSpec((POOL_HALO, D_MODEL), lambda i: (jnp.minimum((cur(i) + 1) * halo_blocks, last), 0)),
                  pg_spec, _pick((len(POOL_WINDOWS), POOL_CH, POOL_CH), j), _pick((1, D_MODEL), j)]
        scratch += [pltpu.VMEM((TOKEN_TILE, D_MODEL), F32)] * 2
        scratch += [pltpu.VMEM((TOKEN_TILE + 4 * POOL_HALO, D_MODEL), F32),
                    pltpu.VMEM((len(POOL_WINDOWS), TOKEN_TILE, LANES), F32)]
    args += [norm_g, wg, wu, wd]
    specs += [g_spec, _pick((D_MODEL, D_FF), *w_idx), _pick((D_MODEL, D_FF), *w_idx), _pick((D_FF, D_MODEL), *w_idx)]
    if post is not None:
        args.append(g2)
        specs.append(g2_spec)
    out_shape, out_specs = [jax.ShapeDtypeStruct((n, D_MODEL), F32)], [tile_out]
    if post == "rows":
        rows = TOKEN_TILE // CHUNK
        out_shape.append(jax.ShapeDtypeStruct((N_PART, n // CHUNK, PW), BF16))
        out_specs.append(pl.BlockSpec((N_PART, rows, PW), lambda i: (0, i, 0)))
        scratch.append(pltpu.VMEM((N_SLAB, TOKEN_TILE, SLAB), F32))
    for w in cast:
        slab = pl.BlockSpec((w.shape[0] // n_tiles, w.shape[1]), lambda i: (i, 0))
        assert w.shape[0] % (n_tiles * BF16_ROWS) == 0
        args.append(w)
        specs.append(slab)
        out_shape.append(jax.ShapeDtypeStruct(w.shape, BF16))
        out_specs.append(slab)
    return pl.pallas_call(
        functools.partial(_ffn_body, pre=pre, post=post, n_cast=len(cast), n_tiles=n_tiles,
                          tiles_per_seq=seq_len // TOKEN_TILE, seq_len=seq_len),
        grid=(n_tiles + 1 if skewed else n_tiles,),
        in_specs=specs, out_specs=out_specs, out_shape=out_shape, scratch_shapes=scratch,
        compiler_params=_params(("arbitrary",) if skewed else ("parallel",)),
    )(*args)


def _cmul(ar, ai, br, bi):
    return ar * br - ai * bi, ar * bi + ai * br


def _dot_nt(a, b):
    return lax.dot_general(a, b, (((1,), (1,)), ((), ())), precision=lax.Precision.HIGHEST, preferred_element_type=F32)


def _s5_prep_part(k, lr_ref, li_ref, ls_ref, br_ref, bi_ref, cr_ref, ci_ref, d_ref, wst_ref, wout_ref, m_ref, a_ref):
    bs, cs = [], []
    for d in range(2):
        lr, li = lr_ref[k, d], li_ref[k, d]
        step = jnp.exp(ls_ref[k, d])
        mag = jnp.exp(lr * step)
        ar, ai = mag * jnp.cos(li * step), mag * jnp.sin(li * step)
        den = lr * lr + li * li
        nr = ar - 1.0
        kr, ki = (nr * lr + ai * li) / den, (ai * lr - nr * li) / den
        bbr, bbi = _cmul(kr, ki, br_ref[k, d], bi_ref[k, d])
        cre, cim = cr_ref[k, d], ci_ref[k, d]
        pr, pi = jnp.ones_like(ar), jnp.zeros_like(ar)
        bs.append([])
        cs.append([])
        for j in range(CHUNK + 1):
            xr, xi = _cmul(pr, pi, bbr, bbi)
            bs[d].append(jnp.concatenate([xr, xi], axis=1))
            yr, yi = _cmul(pr, pi, cre, cim)
            cs[d].append(jnp.concatenate([yr, -yi], axis=1))
            if j == CHUNK:
                a_ref[d, k] = jnp.concatenate([pr, pi], axis=1)
            pr, pi = _cmul(pr, pi, ar, ai)
        toks = [_pos_tok(p, k) for p in range(CHUNK)]
        for p, s in enumerate(toks):
            j = CHUNK - 1 - s if d == 0 else s
            wst_ref[d, k, p * PART:(p + 1) * PART, :] = bs[d][j].astype(BF16)
        wout_t = jnp.concatenate([cs[d][t + 1 if d == 0 else CHUNK - t] for t in toks], axis=0)
        wout_ref[d, k] = wout_t.T.astype(BF16)
    kf = _dot_nt(bs[0][0], jnp.concatenate([cs[0][j] for j in range(CHUNK)], axis=0))
    kb = _dot_nt(bs[1][0], jnp.concatenate([cs[1][CHUNK - 1 - j] for j in range(CHUNK)], axis=0))
    ri = lax.broadcasted_iota(jnp.int32, (PART, PW), 0)
    ci = lax.broadcasted_iota(jnp.int32, (PART, PW), 1)
    for s in range(CHUNK):
        f = kf if s == 0 else pltpu.roll(kf, s * PART, 1)
        b = kb if s == CHUNK - 1 else pltpu.roll(kb, (s + 1) * PART, 1)
        blk = (jnp.where(ci >= s * PART, f, 0.0) + jnp.where(ci < (s + 1) * PART, b, 0.0)
               + jnp.where(ci == ri + s * PART, d_ref[k], 0.0))
        if k:
            blk = jnp.concatenate([pltpu.roll(blk[:, u * LANES:(u + 1) * LANES], k * PART, 1)
                                   for u in range(PW // LANES)], axis=1)
        p = _tok_pos(s, k)
        m_ref[k, p * PART:(p + 1) * PART, :] = blk.astype(BF16)


N_PREP_IN, N_PREP_OUT = 8, 4


def _s5_prep_body(*refs, n_cast):
    ins, refs = refs[:N_PREP_IN], refs[N_PREP_IN:]
    cast_in, refs = refs[:n_cast], refs[n_cast:]
    outs, cast_out = refs[:N_PREP_OUT], refs[N_PREP_OUT:]
    for k in range(PARTS_PER_SLAB):
        _s5_prep_part(k, *ins, *outs)
    for src, dst in zip(cast_in, cast_out):
        dst[...] = src[...].astype(BF16)


def _s5_prep(lam_re, lam_im, log_step, b_re, b_im, c_re, c_im, d, cast=()):
    gpp = PART // S5_GROUP
    eye = jnp.eye(gpp, dtype=F32)

    def state_rows(v):
        return v.reshape(2, N_PART, 1, PSTATE).transpose(1, 0, 2, 3)

    def block_diag(v):
        v = v.transpose(1, 0, 2, 3, 4)
        v = v[:, :, :, :, None, :] * eye[None, None, :, None, :, None]
        return v.reshape(N_PART, 2, PART, PSTATE)

    ls = jnp.broadcast_to(log_step[:, :, None], lam_re.shape)
    bshape = (2, N_PART, gpp, S5_STATE, S5_GROUP)
    cshape = (2, N_PART, gpp, S5_GROUP, S5_STATE)
    args = (state_rows(lam_re), state_rows(lam_im), state_rows(ls),
            block_diag(b_re.reshape(bshape).transpose(0, 1, 2, 4, 3)), block_diag(b_im.reshape(bshape).transpose(0, 1, 2, 4, 3)),
            block_diag(c_re.reshape(cshape)), block_diag(c_im.reshape(cshape)),
            jnp.tile(d.reshape(N_PART, 1, PART), (1, 1, CHUNK)))
    pps = PARTS_PER_SLAB
    vec = pl.BlockSpec((pps, 2, 1, PSTATE), lambda c: (c, 0, 0, 0))
    mat = pl.BlockSpec((pps, 2, PART, PSTATE), lambda c: (c, 0, 0, 0))
    slabs = [(w.shape[2] // N_SLAB, w.shape[3]) for w in cast]
    assert all(w.shape[2] % (N_SLAB * BF16_ROWS) == 0 for w in cast)
    wst, wout, m, a, *own = pl.pallas_call(
        functools.partial(_s5_prep_body, n_cast=len(cast)),
        grid=(N_SLAB,),
        in_specs=[vec, vec, vec, mat, mat, mat, mat, pl.BlockSpec((pps, 1, PW), lambda c: (c, 0, 0))]
                 + [pl.BlockSpec((None, None) + blk, lambda c: (0, 0, c, 0)) for blk in slabs],
        out_specs=[pl.BlockSpec((2, pps, PW, PCOLS), lambda c: (0, c, 0, 0)),
                   pl.BlockSpec((2, pps, PCOLS, PW), lambda c: (0, c, 0, 0)),
                   pl.BlockSpec((pps, PW, PW), lambda c: (c, 0, 0)),
                   pl.BlockSpec((2, pps, 1, PCOLS), lambda c: (0, c, 0, 0))]
                  + [pl.BlockSpec(blk, lambda c: (c, 0)) for blk in slabs],
        out_shape=[jax.ShapeDtypeStruct((2, N_PART, PW, PCOLS), BF16),
                   jax.ShapeDtypeStruct((2, N_PART, PCOLS, PW), BF16),
                   jax.ShapeDtypeStruct((N_PART, PW, PW), BF16),
                   jax.ShapeDtypeStruct((2, N_PART, 1, PCOLS), F32)]
                  + [jax.ShapeDtypeStruct(w.shape[2:], BF16) for w in cast],
        compiler_params=_params(("parallel",)),
    )(*args, *cast)
    return (wst, wout, m, a.reshape(2, N_SET, SUBLANES, PCOLS)), tuple(own)


def _s5_states_body(xf_ref, xb_ref, wst_ref, a_ref, of_ref, ob_ref, *scr, blocks_per_seq, rows):
    x_refs, o_refs = (xf_ref, xb_ref), (of_ref, ob_ref)
    halves = PCOLS // LANES
    n_piece = N_SET * halves
    car_scr = scr[2 * n_piece]
    piece = lambda d, st, ri: scr[d * n_piece + halves * st + ri]

    @pl.when(pl.program_id(0) % blocks_per_seq == 0)
    def _():
        car_scr[...] = jnp.zeros_like(car_scr)

    for d in range(2):
        for q in range(N_PART):
            s_loc = jnp.dot(x_refs[d][q], wst_ref[d, q], preferred_element_type=F32)
            for ri in range(halves):
                piece(d, q // SUBLANES, ri)[pl.ds(q % SUBLANES, rows, stride=SUBLANES), :] = (
                    s_loc[:, ri * LANES:(ri + 1) * LANES])

    def step(n, carry):
        new = []
        for d in range(2):
            r = n if d == 0 else rows - 1 - n
            r0 = pl.multiple_of(r * SUBLANES, SUBLANES)
            for st in range(N_SET):
                sr, si = carry[(d * N_SET + st) * 2], carry[(d * N_SET + st) * 2 + 1]
                ar, ai = a_ref[d, st, :, :LANES], a_ref[d, st, :, LANES:]
                re_scr, im_scr = piece(d, st, 0), piece(d, st, 1)
                xr, xi = re_scr[pl.ds(r0, SUBLANES), :], im_scr[pl.ds(r0, SUBLANES), :]
                re_scr[pl.ds(r0, SUBLANES), :] = sr
                im_scr[pl.ds(r0, SUBLANES), :] = si
                new += [ar * sr - ai * si + xr, ar * si + ai * sr + xi]
        return tuple(new)

    n_car = 2 * n_piece
    carry = lax.fori_loop(0, rows, step, tuple(car_scr[k] for k in range(n_car)), unroll=4)
    for k in range(n_car):
        car_scr[k] = carry[k]
    for d in range(2):
        for q in range(N_PART):
            for ri in range(halves):
                o_refs[d][q, :, ri * LANES:(ri + 1) * LANES] = (
                    piece(d, q // SUBLANES, ri)[pl.ds(q % SUBLANES, rows, stride=SUBLANES), :].astype(BF16))


def _s5_states(hs, wst, a, rows_per_seq):
    nrows = hs.shape[1]
    rows = min(ROW_BLOCK, rows_per_seq)
    nb = nrows // rows
    n_piece = N_SET * (PCOLS // LANES)
    return pl.pallas_call(
        functools.partial(_s5_states_body, blocks_per_seq=rows_per_seq // rows, rows=rows),
        grid=(nb,),
        in_specs=[pl.BlockSpec((N_PART, rows, PW), lambda i: (0, i, 0)),
                  pl.BlockSpec((N_PART, rows, PW), lambda i: (0, nb - 1 - i, 0)),
                  _pick((2, N_PART, PW, PCOLS)), _pick((2, N_SET, SUBLANES, PCOLS))],
        out_specs=[pl.BlockSpec((N_PART, rows, PCOLS), lambda i: (0, i, 0)),
                   pl.BlockSpec((N_PART, rows, PCOLS), lambda i: (0, nb - 1 - i, 0))],
        out_shape=[jax.ShapeDtypeStruct((N_PART, nrows, PCOLS), BF16)] * 2,
        scratch_shapes=[pltpu.VMEM((rows * SUBLANES, LANES), F32)] * (2 * n_piece)
                       + [pltpu.VMEM((2 * n_piece, SUBLANES, LANES), F32)],
        compiler_params=_params(("arbitrary",)),
    )(hs, hs, wst, a)


def _s5_out_body(x_ref, sf_ref, sb_ref, m_ref, wout_ref, o_ref, *, rows):
    for c in range(N_SLAB):
        ys = []
        for k in range(PARTS_PER_SLAB):
            q = PARTS_PER_SLAB * c + k
            y = jnp.dot(x_ref[q], m_ref[q], preferred_element_type=F32)
            y += jnp.dot(sf_ref[q], wout_ref[0, q], preferred_element_type=F32)
            y += jnp.dot(sb_ref[q], wout_ref[1, q], preferred_element_type=F32)
            ys.append(y)
        for u in range(CHUNK // PARTS_PER_SLAB):
            tok = _parts_to_tok([y[:, u * LANES:(u + 1) * LANES] for y in ys])
            for m, piece in enumerate(tok):
                o_ref[c, pl.ds(PARTS_PER_SLAB * u + m, rows, stride=CHUNK), :] = piece


def _s5_out(hs, sf, sb, m, wout, rows_per_seq):
    nrows = hs.shape[1]
    rows = min(ROW_BLOCK, rows_per_seq)
    blk = lambda i: (0, i, 0)
    return pl.pallas_call(
        functools.partial(_s5_out_body, rows=rows),
        grid=(nrows // rows,),
        in_specs=[pl.BlockSpec((N_PART, rows, PW), blk), pl.BlockSpec((N_PART, rows, PCOLS), blk),
                  pl.BlockSpec((N_PART, rows, PCOLS), blk),
                  _pick((N_PART, PW, PW)), _pick((2, N_PART, PCOLS, PW))],
        out_specs=pl.BlockSpec((N_SLAB, rows * CHUNK, SLAB), blk),
        out_shape=jax.ShapeDtypeStruct((N_SLAB, nrows * CHUNK, SLAB), F32),
        compiler_params=_params(("parallel",)),
    )(hs, sf, sb, m, wout)


def _trunk(x, seq_len, norm_g, final_norm_g, w16, s5, pool_scale, first=None):
    depth = norm_g.shape[0]
    gain = lambda layer, k: _pick((1, D_MODEL), layer, k)

    def ffn(x, g_spec, idx, **kw):
        nonlocal w16
        if w16 is not None:
            return _ffn(x, g_spec, idx, norm_g, *w16[:3], seq_len=seq_len, **kw)
        own, stacks = first
        out = _ffn(x, g_spec, (), norm_g, *own, seq_len=seq_len,
                   cast=tuple(w.reshape(-1, w.shape[-1]) for w in stacks), **kw)
        w16 = tuple(w.reshape(f.shape) for w, f in zip(out[-len(stacks):], stacks))
        return out[:-len(stacks)]

    for layer in range(depth):
        j = layer // 2
        last = dict(post="final", g2=final_norm_g, g2_spec=_pick((1, D_MODEL))) if layer == depth - 1 else {}
        if layer % 2 == 0:
            wst, wout, m, a = s5[j]
            x, hs = ffn(x, gain(layer, 0), (layer, 0), post="rows", g2=norm_g, g2_spec=gain(layer, 1))
            sf, sb = _s5_states(hs, wst, a, seq_len // CHUNK)
            y = _s5_out(hs, sf, sb, m, wout, seq_len // CHUNK)
            x = ffn(x, gain(layer, 2), (layer, 1), pre="glu", pre_args=(y, w16[3], w16[4], j), **last)[0]
        else:
            x = ffn(x, gain(layer, 0), (layer, 0))[0]
            x = ffn(x, gain(layer, 2), (layer, 1), pre="pool", pre_args=(gain(layer, 1), w16[5], pool_scale, j), **last)[0]
    return x, w16


def kernel(x_prompt, x_sample, norm_g, final_norm_g, ffn_w_gate, ffn_w_up, ffn_w_down, s5_lambda_re, s5_lambda_im, s5_log_step, s5_b_re, s5_b_im, s5_c_re, s5_c_im, s5_d, s5_w_glu_a, s5_w_glu_b, pool_w, pool_scale):
    stacks = (ffn_w_gate, ffn_w_up, ffn_w_down, s5_w_glu_a, s5_w_glu_b, pool_w)
    norm_g = norm_g.reshape(norm_g.shape[0], 3, 1, D_MODEL)
    final_norm_g = final_norm_g.reshape(1, D_MODEL)
    pool_scale = pool_scale.reshape(-1, 1, D_MODEL)
    s5, own = [], ()
    for j in range(s5_lambda_re.shape[0]):
        mats, own_j = _s5_prep(s5_lambda_re[j], s5_lambda_im[j], s5_log_step[j], s5_b_re[j], s5_b_im[j], s5_c_re[j],
                               s5_c_im[j], s5_d[j], cast=stacks[:3] if j == 0 else ())
        s5.append(mats)
        own = own or own_j
    first = (own, stacks)
    outs, w16 = [], None
    for x in (x_prompt, x_sample):
        bsz, seq, _ = x.shape
        y, w16 = _trunk(x.reshape(bsz * seq, D_MODEL), seq, norm_g, final_norm_g, w16, s5, pool_scale, first)
        outs.append(y.reshape(bsz, seq, D_MODEL))
    return tuple(outs)
```

```python
import functools
import itertools

import jax
import jax.numpy as jnp
from jax import lax
from jax.experimental import pallas as pl
from jax.experimental.pallas import tpu as pltpu

D_MODEL = 1024
D_FF = 2816
RMS_EPS = 1e-6
S5_GROUP = 16
S5_STATE = 64
POOL_WINDOWS = (2, 4, 8, 16)
POOL_CH = D_MODEL // len(POOL_WINDOWS)
POOL_HALO = 8

LANES = 128
SUBLANES = 8
BF16_ROWS = 16
MXU_COLS = 256
SLAB = LANES
N_SLAB = D_MODEL // SLAB
PART = 32
PARTS_PER_SLAB = SLAB // PART
N_PART = D_MODEL // PART
PSTATE = (PART // S5_GROUP) * S5_STATE
PCOLS = 2 * PSTATE
CHUNK = SUBLANES
PW = CHUNK * PART
N_SET = N_PART // SUBLANES
ROW_BLOCK = 128
TOKEN_TILE = 512
VMEM_LIMIT = 56 * 1024 * 1024

F32 = jnp.float32
BF16 = jnp.bfloat16


def _params(sem, vmem=VMEM_LIMIT):
    return pltpu.CompilerParams(dimension_semantics=sem, vmem_limit_bytes=vmem)


def _pick(tail, *lead):
    tail, lead = tuple(tail), tuple(lead)
    return pl.BlockSpec((None,) * len(lead) + tail, lambda *_: lead + (0,) * len(tail), pipeline_mode=pl.Buffered(1))


def _rms(x, g):
    return x * lax.rsqrt(jnp.mean(x * x, axis=-1, keepdims=True) + RMS_EPS) * g


def _tok_pos(t, k):
    return PARTS_PER_SLAB * (t // PARTS_PER_SLAB) + (t % PARTS_PER_SLAB + k) % PARTS_PER_SLAB


def _pos_tok(p, k):
    return PARTS_PER_SLAB * (p // PARTS_PER_SLAB) + (p % PARTS_PER_SLAB - k) % PARTS_PER_SLAB


def _pick_blocks(src, first):
    n = PARTS_PER_SLAB
    blk = lax.broadcasted_iota(jnp.int32, src[0].shape, 1) // PART
    out = src[(first + n - 1) % n]
    for p in range(n - 2, -1, -1):
        out = jnp.where(blk == p, src[(first + p) % n], out)
    return out


def _tok_to_parts(tok):
    n = PARTS_PER_SLAB
    rolled = [tok[0]] + [pltpu.roll(tok[m], m * PART, 1) for m in range(1, n)]
    return [_pick_blocks(rolled, -k) for k in range(n)]


def _parts_to_tok(parts):
    n = PARTS_PER_SLAB
    tok = []
    for m in range(n):
        z = _pick_blocks(parts, -m)
        tok.append(z if m == 0 else pltpu.roll(z, (n - m) * PART, 1))
    return tok


def _pool_pieces(x_ref, prev_ref, next_ref, g, w_ref, scale_ref, e_scr, inv_scr, o_ref, j, tiles_per_seq, seq_len):
    lo, n = 2 * POOL_HALO, TOKEN_TILE + 4 * POOL_HALO
    pad = jnp.zeros((POOL_HALO, D_MODEL), F32)
    e_scr[...] = jnp.concatenate([pad, jnp.where(j == 0, 0.0, _rms(prev_ref[...], g)), _rms(x_ref[...], g),
                                  jnp.where(j == tiles_per_seq - 1, 0.0, _rms(next_ref[...], g)), pad], axis=0)
    pos = j * TOKEN_TILE + lax.broadcasted_iota(jnp.int32, (TOKEN_TILE, 1), 0)
    for gi, win in enumerate(POOL_WINDOWS):
        cnt = jnp.minimum(pos + (win - win // 2), seq_len) - jnp.maximum(pos - win // 2, 0)
        inv_scr[gi] = jnp.broadcast_to(1.0 / cnt.astype(F32), (TOKEN_TILE, LANES))
    yield
    for gi, win in enumerate(POOL_WINDOWS):
        ps = []
        for half in range(POOL_CH // LANES):
            cols = slice(gi * POOL_CH + half * LANES, gi * POOL_CH + (half + 1) * LANES)
            c = e_scr[:, cols]
            w = 1
            while w < win:
                c = c + pltpu.roll(c, w, 0)
                w *= 2
            lead = win - win // 2 - 1
            if lead:
                c = pltpu.roll(c, n - lead, 0)
            ps.append((c[lo:lo + TOKEN_TILE] * inv_scr[gi] - e_scr[lo:lo + TOKEN_TILE, cols]).astype(BF16))
            yield
        cols = slice(gi * POOL_CH, (gi + 1) * POOL_CH)
        z = jnp.dot(jnp.concatenate(ps, axis=1), w_ref[gi], preferred_element_type=F32)
        o_ref[:, cols] = x_ref[:, cols] + z * scale_ref[:, cols]
        yield


def _ffn_core(x, g_ref, wg_ref, wu_ref, wd_ref, between=None):
    h = _rms(x, g_ref[...]).astype(BF16)
    gate = jnp.dot(h, wg_ref[...], preferred_element_type=F32)
    if between is not None:
        between()
    up = jnp.dot(h, wu_ref[...], preferred_element_type=F32)
    act = (gate * jax.nn.sigmoid(gate) * up).astype(BF16)
    return x + 0.5 * jnp.dot(act, wd_ref[...], preferred_element_type=F32)


def _ffn_core_with(side, x_ref, g_ref, wg_ref, wu_ref, wd_ref):
    h = _rms(x_ref[...], g_ref[...]).astype(BF16)
    acts = []
    for c in range(D_FF // MXU_COLS):
        cols = slice(c * MXU_COLS, (c + 1) * MXU_COLS)
        gate = jnp.dot(h, wg_ref[:, cols], preferred_element_type=F32)
        up = jnp.dot(h, wu_ref[:, cols], preferred_element_type=F32)
        acts.append((gate * jax.nn.sigmoid(gate) * up).astype(BF16))
        next(side, None)
    act = jnp.concatenate(acts, axis=1)
    ys = []
    for c in range(D_MODEL // MXU_COLS):
        cols = slice(c * MXU_COLS, (c + 1) * MXU_COLS)
        ys.append(x_ref[:, cols] + 0.5 * jnp.dot(act, wd_ref[:, cols], preferred_element_type=F32))
        next(side, None)
    for _ in side:
        pass
    return jnp.concatenate(ys, axis=1)


def _rows_pieces(h_scr, hs_ref):
    rows = TOKEN_TILE // CHUNK
    for c in range(N_SLAB):
        for u in range(CHUNK // PARTS_PER_SLAB):
            tok = [h_scr[c, pl.ds(PARTS_PER_SLAB * u + m, rows, stride=CHUNK), :] for m in range(PARTS_PER_SLAB)]
            for k, piece in enumerate(_tok_to_parts(tok)):
                hs_ref[PARTS_PER_SLAB * c + k, :, u * LANES:(u + 1) * LANES] = piece.astype(BF16)
        yield


def _by_parity(i, n, first, stage, last, buf0, buf1):
    @pl.when(i == 0)
    def _():
        first(buf0)

    @pl.when((i > 0) & (i < n) & (i % 2 == 0))
    def _():
        stage(buf0, buf1)

    @pl.when((i < n) & (i % 2 == 1))
    def _():
        stage(buf1, buf0)

    @pl.when(i == n)
    def _():
        last(buf1 if n % 2 == 0 else buf0)


def _ffn_body(*refs, pre, post, n_cast, n_tiles, tiles_per_seq, seq_len):
    it = iter(refs)
    x_ref = next(it)
    if pre == "glu":
        y_ref, wa_ref, wb_ref = next(it), next(it), next(it)
    if pre == "pool":
        prev_ref, next_ref, pg_ref, pw_ref, ps_ref = next(it), next(it), next(it), next(it), next(it)
    g_ref, wg_ref, wu_ref, wd_ref = next(it), next(it), next(it), next(it)
    if post is not None:
        g2_ref = next(it)
    cast_in = [next(it) for _ in range(n_cast)]
    o_ref = next(it)
    if post == "rows":
        hs_ref = next(it)
    cast_out = [next(it) for _ in range(n_cast)]
    if pre == "pool":
        buf0, buf1, e_scr, inv_scr = next(it), next(it), next(it), next(it)
    if post == "rows":
        buf0, buf1 = next(it), next(it)
    i = pl.program_id(0)

    def cast_slabs():
        for src, dst in zip(cast_in, cast_out):
            dst[...] = src[...].astype(BF16)

    ffn = functools.partial(_ffn_core, g_ref=g_ref, wg_ref=wg_ref, wu_ref=wu_ref, wd_ref=wd_ref,
                            between=cast_slabs if n_cast else None)

    if pre == "pool":
        def pool(mixed_w):
            j = jnp.minimum(i, n_tiles - 1) % tiles_per_seq
            return _pool_pieces(x_ref, prev_ref, next_ref, pg_ref[...], pw_ref, ps_ref, e_scr, inv_scr, mixed_w,
                                j, tiles_per_seq, seq_len)

        def first(mixed_w):
            for _ in pool(mixed_w):
                pass

        def stage(mixed_w, mixed_r, last=False):
            side = iter(()) if last else pool(mixed_w)
            y = _ffn_core_with(side, mixed_r, g_ref, wg_ref, wu_ref, wd_ref)
            o_ref[...] = _rms(y, g2_ref[...]) if post == "final" else y
        _by_parity(i, n_tiles, first, stage, functools.partial(stage, None, last=True), buf0, buf1)
        return

    def x_in():
        x = x_ref[...]
        if pre == "glu":
            y = jnp.concatenate([y_ref[c] for c in range(N_SLAB)], axis=1)
            gy = jax.nn.gelu(y).astype(BF16)
            a = jnp.dot(gy, wa_ref[...], preferred_element_type=F32)
            b = jnp.dot(gy, wb_ref[...], preferred_element_type=F32)
            x = x + a * jax.nn.sigmoid(b)
        return x

    if post == "rows":
        def cast_piece():
            cast_slabs()
            yield

        def stage(h_w, h_r=None):
            side = cast_piece() if h_r is None else itertools.chain(cast_piece(), _rows_pieces(h_r, hs_ref))
            y = _ffn_core_with(side, x_ref if pre is None else x_in(), g_ref, wg_ref, wu_ref, wd_ref)
            o_ref[...] = y
            h2 = _rms(y, g2_ref[...])
            for c in range(N_SLAB):
                h_w[c] = h2[:, c * SLAB:(c + 1) * SLAB]

        def last(h_r):
            for _ in _rows_pieces(h_r, hs_ref):
                pass
        _by_parity(i, n_tiles, stage, stage, last, buf0, buf1)
        return
    y = ffn(x_in())
    o_ref[...] = _rms(y, g2_ref[...]) if post == "final" else y


def _ffn(x, g_spec, w_idx, norm_g, wg, wu, wd, *, seq_len, pre=None, pre_args=(), post=None, g2=None, g2_spec=None,
         cast=()):
    assert not (pre == "pool" and (post == "rows" or cast)), "the skewed pool stage has no extra outputs"
    n = x.shape[0]
    n_tiles = n // TOKEN_TILE
    skewed = pre == "pool" or post == "rows"
    cur = (lambda i: jnp.minimum(i, n_tiles - 1)) if skewed else (lambda i: i)
    behind = lambda i: jnp.maximum(i - 1, 0)
    tile_in = pl.BlockSpec((TOKEN_TILE, D_MODEL), lambda i: (cur(i), 0))
    tile_out = pl.BlockSpec((TOKEN_TILE, D_MODEL), (lambda i: (behind(i), 0)) if pre == "pool" else (lambda i: (cur(i), 0)))
    args, specs, scratch = [x], [tile_in], []
    if pre == "glu":
        y, wa, wb, j = pre_args
        args += [y, wa, wb]
        specs += [pl.BlockSpec((N_SLAB, TOKEN_TILE, SLAB), lambda i: (0, cur(i), 0)),
                  _pick((D_MODEL, D_MODEL), j), _pick((D_MODEL, D_MODEL), j)]
    if pre == "pool":
        pg_spec, pw, ps, j = pre_args
        halo_blocks = TOKEN_TILE // POOL_HALO
        last = n // POOL_HALO - 1
        args += [x, x, norm_g, pw, ps]
        specs += [pl.BlockSpec((POOL_HALO, D_MODEL), lambda i: (jnp.maximum(cur(i) * halo_blocks - 1, 0), 0)),
                  pl.BlockSpec((POOL_HALO, D_MODEL), lambda i: (jnp.minimum((cur(i) + 1) * halo_blocks, last), 0)),
                  pg_spec, _pick((len(POOL_WINDOWS), POOL_CH, POOL_CH), j), _pick((1, D_MODEL), j)]
        scratch += [pltpu.VMEM((TOKEN_TILE, D_MODEL), F32)] * 2
        scratch += [pltpu.VMEM((TOKEN_TILE + 4 * POOL_HALO, D_MODEL), F32),
                    pltpu.VMEM((len(POOL_WINDOWS), TOKEN_TILE, LANES), F32)]
    args += [norm_g, wg, wu, wd]
    specs += [g_spec, _pick((D_MODEL, D_FF), *w_idx), _pick((D_MODEL, D_FF), *w_idx), _pick((D_FF, D_MODEL), *w_idx)]
    if post is not None:
        args.append(g2)
        specs.append(g2_spec)
    out_shape, out_specs = [jax.ShapeDtypeStruct((n, D_MODEL), F32)], [tile_out]
    if post == "rows":
        rows = TOKEN_TILE // CHUNK
        out_shape.append(jax.ShapeDtypeStruct((N_PART, n // CHUNK, PW), BF16))
        out_specs.append(pl.BlockSpec((N_PART, rows, PW), lambda i: (0, behind(i), 0)))
        scratch += [pltpu.VMEM((N_SLAB, TOKEN_TILE, SLAB), F32)] * 2
    for w in cast:
        slab = pl.BlockSpec((w.shape[0] // n_tiles, w.shape[1]), lambda i: (cur(i), 0))
        assert w.shape[0] % (n_tiles * BF16_ROWS) == 0
        args.append(w)
        specs.append(slab)
        out_shape.append(jax.ShapeDtypeStruct(w.shape, BF16))
        out_specs.append(slab)
    return pl.pallas_call(
        functools.partial(_ffn_body, pre=pre, post=post, n_cast=len(cast), n_tiles=n_tiles,
                          tiles_per_seq=seq_len // TOKEN_TILE, seq_len=seq_len),
        grid=(n_tiles + 1 if skewed else n_tiles,),
        in_specs=specs, out_specs=out_specs, out_shape=out_shape, scratch_shapes=scratch,
        compiler_params=_params(("arbitrary",) if skewed else ("parallel",)),
    )(*args)


def _cmul(ar, ai, br, bi):
    return ar * br - ai * bi, ar * bi + ai * br


def _dot_nt(a, b):
    return lax.dot_general(a, b, (((1,), (1,)), ((), ())), precision=lax.Precision.HIGHEST, preferred_element_type=F32)


def _s5_prep_part(k, lr_ref, li_ref, ls_ref, br_ref, bi_ref, cr_ref, ci_ref, d_ref, wst_ref, wout_ref, m_ref, a_ref):
    bs, cs = [], []
    for d in range(2):
        lr, li = lr_ref[k, d], li_ref[k, d]
        step = jnp.exp(ls_ref[k, d])
        mag = jnp.exp(lr * step)
        ar, ai = mag * jnp.cos(li * step), mag * jnp.sin(li * step)
        den = lr * lr + li * li
        nr = ar - 1.0
        kr, ki = (nr * lr + ai * li) / den, (ai * lr - nr * li) / den
        bbr, bbi = _cmul(kr, ki, br_ref[k, d], bi_ref[k, d])
        cre, cim = cr_ref[k, d], ci_ref[k, d]
        pr, pi = jnp.ones_like(ar), jnp.zeros_like(ar)
        bs.append([])
        cs.append([])
        for j in range(CHUNK + 1):
            xr, xi = _cmul(pr, pi, bbr, bbi)
            bs[d].append(jnp.concatenate([xr, xi], axis=1))
            yr, yi = _cmul(pr, pi, cre, cim)
            cs[d].append(jnp.concatenate([yr, -yi], axis=1))
            if j == CHUNK:
                a_ref[d, k] = jnp.concatenate([pr, pi], axis=1)
            pr, pi = _cmul(pr, pi, ar, ai)
        toks = [_pos_tok(p, k) for p in range(CHUNK)]
        for p, s in enumerate(toks):
            j = CHUNK - 1 - s if d == 0 else s
            wst_ref[d, k, p * PART:(p + 1) * PART, :] = bs[d][j].astype(BF16)
        wout_t = jnp.concatenate([cs[d][t + 1 if d == 0 else CHUNK - t] for t in toks], axis=0)
        wout_ref[d, k] = wout_t.T.astype(BF16)
    kf = _dot_nt(bs[0][0], jnp.concatenate([cs[0][j] for j in range(CHUNK)], axis=0))
    kb = _dot_nt(bs[1][0], jnp.concatenate([cs[1][CHUNK - 1 - j] for j in range(CHUNK)], axis=0))
    ri = lax.broadcasted_iota(jnp.int32, (PART, PW), 0)
    ci = lax.broadcasted_iota(jnp.int32, (PART, PW), 1)
    for s in range(CHUNK):
        f = kf if s == 0 else pltpu.roll(kf, s * PART, 1)
        b = kb if s == CHUNK - 1 else pltpu.roll(kb, (s + 1) * PART, 1)
        blk = (jnp.where(ci >= s * PART, f, 0.0) + jnp.where(ci < (s + 1) * PART, b, 0.0)
               + jnp.where(ci == ri + s * PART, d_ref[k], 0.0))
        if k:
            blk = jnp.concatenate([pltpu.roll(blk[:, u * LANES:(u + 1) * LANES], k * PART, 1)
                                   for u in range(PW // LANES)], axis=1)
        p = _tok_pos(s, k)
        m_ref[k, p * PART:(p + 1) * PART, :] = blk.astype(BF16)


N_PREP_IN, N_PREP_OUT = 8, 4


def _s5_prep_body(*refs, n_cast):
    ins, refs = refs[:N_PREP_IN], refs[N_PREP_IN:]
    cast_in, refs = refs[:n_cast], refs[n_cast:]
    outs, cast_out = refs[:N_PREP_OUT], refs[N_PREP_OUT:]
    for k in range(PARTS_PER_SLAB):
        _s5_prep_part(k, *ins, *outs)
    for src, dst in zip(cast_in, cast_out):
        dst[...] = src[...].astype(BF16)


def _s5_prep(lam_re, lam_im, log_step, b_re, b_im, c_re, c_im, d, cast=()):
    gpp = PART // S5_GROUP
    eye = jnp.eye(gpp, dtype=F32)

    def state_rows(v):
        return v.reshape(2, N_PART, 1, PSTATE).transpose(1, 0, 2, 3)

    def block_diag(v):
        v = v.transpose(1, 0, 2, 3, 4)
        v = v[:, :, :, :, None, :] * eye[None, None, :, None, :, None]
        return v.reshape(N_PART, 2, PART, PSTATE)

    ls = jnp.broadcast_to(log_step[:, :, None], lam_re.shape)
    bshape = (2, N_PART, gpp, S5_STATE, S5_GROUP)
    cshape = (2, N_PART, gpp, S5_GROUP, S5_STATE)
    args = (state_rows(lam_re), state_rows(lam_im), state_rows(ls),
            block_diag(b_re.reshape(bshape).transpose(0, 1, 2, 4, 3)), block_diag(b_im.reshape(bshape).transpose(0, 1, 2, 4, 3)),
            block_diag(c_re.reshape(cshape)), block_diag(c_im.reshape(cshape)),
            jnp.tile(d.reshape(N_PART, 1, PART), (1, 1, CHUNK)))
    pps = PARTS_PER_SLAB
    vec = pl.BlockSpec((pps, 2, 1, PSTATE), lambda c: (c, 0, 0, 0))
    mat = pl.BlockSpec((pps, 2, PART, PSTATE), lambda c: (c, 0, 0, 0))
    slabs = [(w.shape[2] // N_SLAB, w.shape[3]) for w in cast]
    assert all(w.shape[2] % (N_SLAB * BF16_ROWS) == 0 for w in cast)
    wst, wout, m, a, *own = pl.pallas_call(
        functools.partial(_s5_prep_body, n_cast=len(cast)),
        grid=(N_SLAB,),
        in_specs=[vec, vec, vec, mat, mat, mat, mat, pl.BlockSpec((pps, 1, PW), lambda c: (c, 0, 0))]
                 + [pl.BlockSpec((None, None) + blk, lambda c: (0, 0, c, 0)) for blk in slabs],
        out_specs=[pl.BlockSpec((2, pps, PW, PCOLS), lambda c: (0, c, 0, 0)),
                   pl.BlockSpec((2, pps, PCOLS, PW), lambda c: (0, c, 0, 0)),
                   pl.BlockSpec((pps, PW, PW), lambda c: (c, 0, 0)),
                   pl.BlockSpec((2, pps, 1, PCOLS), lambda c: (0, c, 0, 0))]
                  + [pl.BlockSpec(blk, lambda c: (c, 0)) for blk in slabs],
        out_shape=[jax.ShapeDtypeStruct((2, N_PART, PW, PCOLS), BF16),
                   jax.ShapeDtypeStruct((2, N_PART, PCOLS, PW), BF16),
                   jax.ShapeDtypeStruct((N_PART, PW, PW), BF16),
                   jax.ShapeDtypeStruct((2, N_PART, 1, PCOLS), F32)]
                  + [jax.ShapeDtypeStruct(w.shape[2:], BF16) for w in cast],
        compiler_params=_params(("parallel",)),
    )(*args, *cast)
    return (wst, wout, m, a.reshape(2, N_SET, SUBLANES, PCOLS)), tuple(own)


def _s5_states_body(xf_ref, xb_ref, wst_ref, a_ref, of_ref, ob_ref, *scr, blocks_per_seq, rows):
    x_refs, o_refs = (xf_ref, xb_ref), (of_ref, ob_ref)
    halves = PCOLS // LANES
    n_piece = N_SET * halves
    car_scr = scr[2 * n_piece]
    piece = lambda d, st, ri: scr[d * n_piece + halves * st + ri]

    @pl.when(pl.program_id(0) % blocks_per_seq == 0)
    def _():
        car_scr[...] = jnp.zeros_like(car_scr)

    for d in range(2):
        for q in range(N_PART):
            s_loc = jnp.dot(x_refs[d][q], wst_ref[d, q], preferred_element_type=F32)
            for ri in range(halves):
                piece(d, q // SUBLANES, ri)[pl.ds(q % SUBLANES, rows, stride=SUBLANES), :] = (
                    s_loc[:, ri * LANES:(ri + 1) * LANES])

    def step(n, carry):
        new = []
        for d in range(2):
            r = n if d == 0 else rows - 1 - n
            r0 = pl.multiple_of(r * SUBLANES, SUBLANES)
            for st in range(N_SET):
                sr, si = carry[(d * N_SET + st) * 2], carry[(d * N_SET + st) * 2 + 1]
                ar, ai = a_ref[d, st, :, :LANES], a_ref[d, st, :, LANES:]
                re_scr, im_scr = piece(d, st, 0), piece(d, st, 1)
                xr, xi = re_scr[pl.ds(r0, SUBLANES), :], im_scr[pl.ds(r0, SUBLANES), :]
                re_scr[pl.ds(r0, SUBLANES), :] = sr
                im_scr[pl.ds(r0, SUBLANES), :] = si
                new += [ar * sr - ai * si + xr, ar * si + ai * sr + xi]
        return tuple(new)

    n_car = 2 * n_piece
    carry = lax.fori_loop(0, rows, step, tuple(car_scr[k] for k in range(n_car)), unroll=4)
    for k in range(n_car):
        car_scr[k] = carry[k]
    for d in range(2):
        for q in range(N_PART):
            for ri in range(halves):
                o_refs[d][q, :, ri * LANES:(ri + 1) * LANES] = (
                    piece(d, q // SUBLANES, ri)[pl.ds(q % SUBLANES, rows, stride=SUBLANES), :].astype(BF16))


def _s5_states(hs, wst, a, rows_per_seq):
    nrows = hs.shape[1]
    rows = min(ROW_BLOCK, rows_per_seq)
    nb = nrows // rows
    n_piece = N_SET * (PCOLS // LANES)
    return pl.pallas_call(
        functools.partial(_s5_states_body, blocks_per_seq=rows_per_seq // rows, rows=rows),
        grid=(nb,),
        in_specs=[pl.BlockSpec((N_PART, rows, PW), lambda i: (0, i, 0)),
                  pl.BlockSpec((N_PART, rows, PW), lambda i: (0, nb - 1 - i, 0)),
                  _pick((2, N_PART, PW, PCOLS)), _pick((2, N_SET, SUBLANES, PCOLS))],
        out_specs=[pl.BlockSpec((N_PART, rows, PCOLS), lambda i: (0, i, 0)),
                   pl.BlockSpec((N_PART, rows, PCOLS), lambda i: (0, nb - 1 - i, 0))],
        out_shape=[jax.ShapeDtypeStruct((N_PART, nrows, PCOLS), BF16)] * 2,
        scratch_shapes=[pltpu.VMEM((rows * SUBLANES, LANES), F32)] * (2 * n_piece)
                       + [pltpu.VMEM((2 * n_piece, SUBLANES, LANES), F32)],
        compiler_params=_params(("arbitrary",)),
    )(hs, hs, wst, a)


def _s5_out_body(x_ref, sf_ref, sb_ref, m_ref, wout_ref, o_ref, *, rows):
    for c in range(N_SLAB):
        ys = []
        for k in range(PARTS_PER_SLAB):
            q = PARTS_PER_SLAB * c + k
            y = jnp.dot(x_ref[q], m_ref[q], preferred_element_type=F32)
            y += jnp.dot(sf_ref[q], wout_ref[0, q], preferred_element_type=F32)
            y += jnp.dot(sb_ref[q], wout_ref[1, q], preferred_element_type=F32)
            ys.append(y)
        for u in range(CHUNK // PARTS_PER_SLAB):
            tok = _parts_to_tok([y[:, u * LANES:(u + 1) * LANES] for y in ys])
            for m, piece in enumerate(tok):
                o_ref[c, pl.ds(PARTS_PER_SLAB * u + m, rows, stride=CHUNK), :] = piece


def _s5_out(hs, sf, sb, m, wout, rows_per_seq):
    nrows = hs.shape[1]
    rows = min(ROW_BLOCK, rows_per_seq)
    blk = lambda i: (0, i, 0)
    return pl.pallas_call(
        functools.partial(_s5_out_body, rows=rows),
        grid=(nrows // rows,),
        in_specs=[pl.BlockSpec((N_PART, rows, PW), blk), pl.BlockSpec((N_PART, rows, PCOLS), blk),
                  pl.BlockSpec((N_PART, rows, PCOLS), blk),
                  _pick((N_PART, PW, PW)), _pick((2, N_PART, PCOLS, PW))],
        out_specs=pl.BlockSpec((N_SLAB, rows * CHUNK, SLAB), blk),
        out_shape=jax.ShapeDtypeStruct((N_SLAB, nrows * CHUNK, SLAB), F32),
        compiler_params=_params(("parallel",)),
    )(hs, sf, sb, m, wout)


def _trunk(x, seq_len, norm_g, final_norm_g, w16, s5, pool_scale, first=None):
    depth = norm_g.shape[0]
    gain = lambda layer, k: _pick((1, D_MODEL), layer, k)

    def ffn(x, g_spec, idx, **kw):
        nonlocal w16
        if w16 is not None:
            return _ffn(x, g_spec, idx, norm_g, *w16[:3], seq_len=seq_len, **kw)
        own, stacks = first
        out = _ffn(x, g_spec, (), norm_g, *own, seq_len=seq_len,
                   cast=tuple(w.reshape(-1, w.shape[-1]) for w in stacks), **kw)
        w16 = tuple(w.reshape(f.shape) for w, f in zip(out[-len(stacks):], stacks))
        return out[:-len(stacks)]

    for layer in range(depth):
        j = layer // 2
        last = dict(post="final", g2=final_norm_g, g2_spec=_pick((1, D_MODEL))) if layer == depth - 1 else {}
        if layer % 2 == 0:
            wst, wout, m, a = s5[j]
            x, hs = ffn(x, gain(layer, 0), (layer, 0), post="rows", g2=norm_g, g2_spec=gain(layer, 1))
            sf, sb = _s5_states(hs, wst, a, seq_len // CHUNK)
            y = _s5_out(hs, sf, sb, m, wout, seq_len // CHUNK)
            x = ffn(x, gain(layer, 2), (layer, 1), pre="glu", pre_args=(y, w16[3], w16[4], j), **last)[0]
        else:
            x = ffn(x, gain(layer, 0), (layer, 0))[0]
            x = ffn(x, gain(layer, 2), (layer, 1), pre="pool", pre_args=(gain(layer, 1), w16[5], pool_scale, j), **last)[0]
    return x, w16


def kernel(x_prompt, x_sample, norm_g, final_norm_g, ffn_w_gate, ffn_w_up, ffn_w_down, s5_lambda_re, s5_lambda_im, s5_log_step, s5_b_re, s5_b_im, s5_c_re, s5_c_im, s5_d, s5_w_glu_a, s5_w_glu_b, pool_w, pool_scale):
    stacks = (ffn_w_gate, ffn_w_up, ffn_w_down, s5_w_glu_a, s5_w_glu_b, pool_w)
    norm_g = norm_g.reshape(norm_g.shape[0], 3, 1, D_MODEL)
    final_norm_g = final_norm_g.reshape(1, D_MODEL)
    pool_scale = pool_scale.reshape(-1, 1, D_MODEL)
    s5, own = [], ()
    for j in range(s5_lambda_re.shape[0]):
        mats, own_j = _s5_prep(s5_lambda_re[j], s5_lambda_im[j], s5_log_step[j], s5_b_re[j], s5_b_im[j], s5_c_re[j],
                               s5_c_im[j], s5_d[j], cast=stacks[:3] if j == 0 else ())
        s5.append(mats)
        own = own or own_j
    first = (own, stacks)
    outs, w16 = [], None
    for x in (x_prompt, x_sample):
        bsz, seq, _ = x.shape
        y, w16 = _trunk(x.reshape(bsz * seq, D_MODEL), seq, norm_g, final_norm_g, w16, s5, pool_scale, first)
        outs.append(y.reshape(bsz, seq, D_MODEL))
    return tuple(outs)
```

```python
import functools
import itertools

import jax
import jax.numpy as jnp
from jax import lax
from jax.experimental import pallas as pl
from jax.experimental.pallas import tpu as pltpu

D_MODEL = 1024
D_FF = 2816
RMS_EPS = 1e-6
S5_GROUP = 16
S5_STATE = 64
POOL_WINDOWS = (2, 4, 8, 16)
POOL_CH = D_MODEL // len(POOL_WINDOWS)
POOL_HALO = 8

LANES = 128
SUBLANES = 8
BF16_ROWS = 16
MXU_COLS = 256
SLAB = LANES
N_SLAB = D_MODEL // SLAB
PART = 32
PARTS_PER_SLAB = SLAB // PART
N_PART = D_MODEL // PART
PSTATE = (PART // S5_GROUP) * S5_STATE
PCOLS = 2 * PSTATE
CHUNK = SUBLANES
PW = CHUNK * PART
N_SET = N_PART // SUBLANES
ROW_BLOCK = 128
TOKEN_TILE = 512
VMEM_LIMIT = 56 * 1024 * 1024

F32 = jnp.float32
BF16 = jnp.bfloat16


def _params(sem, vmem=VMEM_LIMIT):
    return pltpu.CompilerParams(dimension_semantics=sem, vmem_limit_bytes=vmem)


def _pick(tail, *lead):
    tail, lead = tuple(tail), tuple(lead)
    return pl.BlockSpec((None,) * len(lead) + tail, lambda *_: lead + (0,) * len(tail), pipeline_mode=pl.Buffered(1))


def _rms(x, g):
    return x * lax.rsqrt(jnp.mean(x * x, axis=-1, keepdims=True) + RMS_EPS) * g


def _tok_pos(t, k):
    return PARTS_PER_SLAB * (t // PARTS_PER_SLAB) + (t % PARTS_PER_SLAB + k) % PARTS_PER_SLAB


def _pos_tok(p, k):
    return PARTS_PER_SLAB * (p // PARTS_PER_SLAB) + (p % PARTS_PER_SLAB - k) % PARTS_PER_SLAB


def _pick_blocks(src, first):
    n = PARTS_PER_SLAB
    blk = lax.broadcasted_iota(jnp.int32, src[0].shape, 1) // PART
    out = src[(first + n - 1) % n]
    for p in range(n - 2, -1, -1):
        out = jnp.where(blk == p, src[(first + p) % n], out)
    return out


def _tok_to_parts(tok):
    n = PARTS_PER_SLAB
    rolled = [tok[0]] + [pltpu.roll(tok[m], m * PART, 1) for m in range(1, n)]
    return [_pick_blocks(rolled, -k) for k in range(n)]


def _parts_to_tok(parts):
    n = PARTS_PER_SLAB
    tok = []
    for m in range(n):
        z = _pick_blocks(parts, -m)
        tok.append(z if m == 0 else pltpu.roll(z, (n - m) * PART, 1))
    return tok


def _pool_pieces(x_ref, prev_ref, next_ref, g, w_ref, scale_ref, e_scr, inv_scr, o_ref, j, tiles_per_seq, seq_len):
    lo, n = 2 * POOL_HALO, TOKEN_TILE + 4 * POOL_HALO
    pad = jnp.zeros((POOL_HALO, D_MODEL), F32)
    e_scr[...] = jnp.concatenate([pad, jnp.where(j == 0, 0.0, _rms(prev_ref[...], g)), _rms(x_ref[...], g),
                                  jnp.where(j == tiles_per_seq - 1, 0.0, _rms(next_ref[...], g)), pad], axis=0)
    pos = j * TOKEN_TILE + lax.broadcasted_iota(jnp.int32, (TOKEN_TILE, 1), 0)
    for gi, win in enumerate(POOL_WINDOWS):
        cnt = jnp.minimum(pos + (win - win // 2), seq_len) - jnp.maximum(pos - win // 2, 0)
        inv_scr[gi] = jnp.broadcast_to(1.0 / cnt.astype(F32), (TOKEN_TILE, LANES))
    yield
    for gi, win in enumerate(POOL_WINDOWS):
        ps = []
        for half in range(POOL_CH // LANES):
            cols = slice(gi * POOL_CH + half * LANES, gi * POOL_CH + (half + 1) * LANES)
            c = e_scr[:, cols]
            w = 1
            while w < win:
                c = c + pltpu.roll(c, w, 0)
                w *= 2
            lead = win - win // 2 - 1
            if lead:
                c = pltpu.roll(c, n - lead, 0)
            ps.append((c[lo:lo + TOKEN_TILE] * inv_scr[gi] - e_scr[lo:lo + TOKEN_TILE, cols]).astype(BF16))
            yield
        cols = slice(gi * POOL_CH, (gi + 1) * POOL_CH)
        z = jnp.dot(jnp.concatenate(ps, axis=1), w_ref[gi], preferred_element_type=F32)
        o_ref[:, cols] = x_ref[:, cols] + z * scale_ref[:, cols]
        yield


def _ffn_core(x, g_ref, wg_ref, wu_ref, wd_ref, between=None):
    h = _rms(x, g_ref[...]).astype(BF16)
    gate = jnp.dot(h, wg_ref[...], preferred_element_type=F32)
    if between is not None:
        between()
    up = jnp.dot(h, wu_ref[...], preferred_element_type=F32)
    act = (gate * jax.nn.sigmoid(gate) * up).astype(BF16)
    return x + 0.5 * jnp.dot(act, wd_ref[...], preferred_element_type=F32)


def _ffn_core_with(side, x_ref, g_ref, wg_ref, wu_ref, wd_ref):
    h = _rms(x_ref[...], g_ref[...]).astype(BF16)
    acts = []
    for c in range(D_FF // MXU_COLS):
        cols = slice(c * MXU_COLS, (c + 1) * MXU_COLS)
        gate = jnp.dot(h, wg_ref[:, cols], preferred_element_type=F32)
        up = jnp.dot(h, wu_ref[:, cols], preferred_element_type=F32)
        acts.append((gate * jax.nn.sigmoid(gate) * up).astype(BF16))
        next(side, None)
    act = jnp.concatenate(acts, axis=1)
    ys = []
    for c in range(D_MODEL // MXU_COLS):
        cols = slice(c * MXU_COLS, (c + 1) * MXU_COLS)
        ys.append(x_ref[:, cols] + 0.5 * jnp.dot(act, wd_ref[:, cols], preferred_element_type=F32))
        next(side, None)
    for _ in side:
        pass
    return jnp.concatenate(ys, axis=1)


def _rows_pieces(h_scr, hs_ref):
    rows = TOKEN_TILE // CHUNK
    for c in range(N_SLAB):
        for u in range(CHUNK // PARTS_PER_SLAB):
            tok = [h_scr[c, pl.ds(PARTS_PER_SLAB * u + m, rows, stride=CHUNK), :] for m in range(PARTS_PER_SLAB)]
            for k, piece in enumerate(_tok_to_parts(tok)):
                hs_ref[PARTS_PER_SLAB * c + k, :, u * LANES:(u + 1) * LANES] = piece.astype(BF16)
        yield


def _by_parity(i, n, first, stage, last, buf0, buf1):
    @pl.when(i == 0)
    def _():
        if first is None:
            buf1[...] = jnp.zeros_like(buf1)
        else:
            first(buf0)

    staged = (i < n) if last is not None else (i <= n)
    even = staged & (i % 2 == 0)
    if first is not None:
        even = even & (i > 0)

    @pl.when(even)
    def _():
        stage(buf0, buf1)

    @pl.when(staged & (i % 2 == 1))
    def _():
        stage(buf1, buf0)

    if last is not None:
        @pl.when(i == n)
        def _():
            last(buf1 if n % 2 == 0 else buf0)


def _ffn_body(*refs, pre, post, n_cast, n_tiles, tiles_per_seq, seq_len):
    it = iter(refs)
    x_ref = next(it)
    if pre == "glu":
        y_ref, wa_ref, wb_ref = next(it), next(it), next(it)
    if pre == "pool":
        prev_ref, next_ref, pg_ref, pw_ref, ps_ref = next(it), next(it), next(it), next(it), next(it)
    g_ref, wg_ref, wu_ref, wd_ref = next(it), next(it), next(it), next(it)
    if post is not None:
        g2_ref = next(it)
    cast_in = [next(it) for _ in range(n_cast)]
    o_ref = next(it)
    if post == "rows":
        hs_ref = next(it)
    cast_out = [next(it) for _ in range(n_cast)]
    if pre == "pool":
        buf0, buf1, e_scr, inv_scr = next(it), next(it), next(it), next(it)
    if post == "rows":
        buf0, buf1 = next(it), next(it)
    i = pl.program_id(0)

    def cast_slabs():
        for src, dst in zip(cast_in, cast_out):
            dst[...] = src[...].astype(BF16)

    ffn = functools.partial(_ffn_core, g_ref=g_ref, wg_ref=wg_ref, wu_ref=wu_ref, wd_ref=wd_ref,
                            between=cast_slabs if n_cast else None)

    if pre == "pool":
        def pool(mixed_w):
            j = jnp.minimum(i, n_tiles - 1) % tiles_per_seq
            return _pool_pieces(x_ref, prev_ref, next_ref, pg_ref[...], pw_ref, ps_ref, e_scr, inv_scr, mixed_w,
                                j, tiles_per_seq, seq_len)

        def first(mixed_w):
            for _ in pool(mixed_w):
                pass

        def stage(mixed_w, mixed_r):
            y = _ffn_core_with(pool(mixed_w), mixed_r, g_ref, wg_ref, wu_ref, wd_ref)
            o_ref[...] = _rms(y, g2_ref[...]) if post == "final" else y
        _by_parity(i, n_tiles, first, stage, None, buf0, buf1)
        return

    def x_in():
        x = x_ref[...]
        if pre == "glu":
            y = jnp.concatenate([y_ref[c] for c in range(N_SLAB)], axis=1)
            gy = jax.nn.gelu(y).astype(BF16)
            a = jnp.dot(gy, wa_ref[...], preferred_element_type=F32)
            b = jnp.dot(gy, wb_ref[...], preferred_element_type=F32)
            x = x + a * jax.nn.sigmoid(b)
        return x

    if post == "rows":
        def cast_piece():
            cast_slabs()
            yield

        def stage(h_w, h_r):
            side = itertools.chain(cast_piece(), _rows_pieces(h_r, hs_ref))
            y = _ffn_core_with(side, x_ref if pre is None else x_in(), g_ref, wg_ref, wu_ref, wd_ref)
            o_ref[...] = y
            h2 = _rms(y, g2_ref[...])
            for c in range(N_SLAB):
                h_w[c] = h2[:, c * SLAB:(c + 1) * SLAB]

        def last(h_r):
            for _ in _rows_pieces(h_r, hs_ref):
                pass
        _by_parity(i, n_tiles, None, stage, last, buf0, buf1)
        return
    y = ffn(x_in())
    o_ref[...] = _rms(y, g2_ref[...]) if post == "final" else y


def _ffn(x, g_spec, w_idx, norm_g, wg, wu, wd, *, seq_len, pre=None, pre_args=(), post=None, g2=None, g2_spec=None,
         cast=()):
    assert not (pre == "pool" and (post == "rows" or cast)), "the skewed pool stage has no extra outputs"
    n = x.shape[0]
    n_tiles = n // TOKEN_TILE
    skewed = pre == "pool" or post == "rows"
    cur = (lambda i: jnp.minimum(i, n_tiles - 1)) if skewed else (lambda i: i)
    behind = lambda i: jnp.maximum(i - 1, 0)
    tile_in = pl.BlockSpec((TOKEN_TILE, D_MODEL), lambda i: (cur(i), 0))
    tile_out = pl.BlockSpec((TOKEN_TILE, D_MODEL), (lambda i: (behind(i), 0)) if pre == "pool" else (lambda i: (cur(i), 0)))
    args, specs, scratch = [x], [tile_in], []
    if pre == "glu":
        y, wa, wb, j = pre_args
        args += [y, wa, wb]
        specs += [pl.BlockSpec((N_SLAB, TOKEN_TILE, SLAB), lambda i: (0, cur(i), 0)),
                  _pick((D_MODEL, D_MODEL), j), _pick((D_MODEL, D_MODEL), j)]
    if pre == "pool":
        pg_spec, pw, ps, j = pre_args
        halo_blocks = TOKEN_TILE // POOL_HALO
        last = n // POOL_HALO - 1
        args += [x, x, norm_g, pw, ps]
        specs += [pl.BlockSpec((POOL_HALO, D_MODEL), lambda i: (jnp.maximum(cur(i) * halo_blocks - 1, 0), 0)),
                  pl.BlockSpec((POOL_HALO, D_MODEL), lambda i: (jnp.minimum((cur(i) + 1) * halo_blocks, last), 0)),
                  pg_spec, _pick((len(POOL_WINDOWS), POOL_CH, POOL_CH), j), _pick((1, D_MODEL), j)]
        scratch += [pltpu.VMEM((TOKEN_TILE, D_MODEL), F32)] * 2
        scratch += [pltpu.VMEM((TOKEN_TILE + 4 * POOL_HALO, D_MODEL), F32),
                    pltpu.VMEM((len(POOL_WINDOWS), TOKEN_TILE, LANES), F32)]
    args += [norm_g, wg, wu, wd]
    specs += [g_spec, _pick((D_MODEL, D_FF), *w_idx), _pick((D_MODEL, D_FF), *w_idx), _pick((D_FF, D_MODEL), *w_idx)]
    if post is not None:
        args.append(g2)
        specs.append(g2_spec)
    out_shape, out_specs = [jax.ShapeDtypeStruct((n, D_MODEL), F32)], [tile_out]
    if post == "rows":
        rows = TOKEN_TILE // CHUNK
        out_shape.append(jax.ShapeDtypeStruct((N_PART, n // CHUNK, PW), BF16))
        out_specs.append(pl.BlockSpec((N_PART, rows, PW), lambda i: (0, behind(i), 0)))
        scratch += [pltpu.VMEM((N_SLAB, TOKEN_TILE, SLAB), F32)] * 2
    for w in cast:
        slab = pl.BlockSpec((w.shape[0] // n_tiles, w.shape[1]), lambda i: (cur(i), 0))
        assert w.shape[0] % (n_tiles * BF16_ROWS) == 0
        args.append(w)
        specs.append(slab)
        out_shape.append(jax.ShapeDtypeStruct(w.shape, BF16))
        out_specs.append(slab)
    return pl.pallas_call(
        functools.partial(_ffn_body, pre=pre, post=post, n_cast=len(cast), n_tiles=n_tiles,
                          tiles_per_seq=seq_len // TOKEN_TILE, seq_len=seq_len),
        grid=(n_tiles + 1 if skewed else n_tiles,),
        in_specs=specs, out_specs=out_specs, out_shape=out_shape, scratch_shapes=scratch,
        compiler_params=_params(("arbitrary",) if skewed else ("parallel",)),
    )(*args)


def _cmul(ar, ai, br, bi):
    return ar * br - ai * bi, ar * bi + ai * br


def _dot_nt(a, b):
    return lax.dot_general(a, b, (((1,), (1,)), ((), ())), precision=lax.Precision.HIGHEST, preferred_element_type=F32)


def _s5_prep_part(k, lr_ref, li_ref, ls_ref, br_ref, bi_ref, cr_ref, ci_ref, d_ref, wst_ref, wout_ref, m_ref, a_ref):
    bs, cs = [], []
    for d in range(2):
        lr, li = lr_ref[k, d], li_ref[k, d]
        step = jnp.exp(ls_ref[k, d])
        mag = jnp.exp(lr * step)
        ar, ai = mag * jnp.cos(li * step), mag * jnp.sin(li * step)
        den = lr * lr + li * li
        nr = ar - 1.0
        kr, ki = (nr * lr + ai * li) / den, (ai * lr - nr * li) / den
        bbr, bbi = _cmul(kr, ki, br_ref[k, d], bi_ref[k, d])
        cre, cim = cr_ref[k, d], ci_ref[k, d]
        pr, pi = jnp.ones_like(ar), jnp.zeros_like(ar)
        bs.append([])
        cs.append([])
        for j in range(CHUNK + 1):
            xr, xi = _cmul(pr, pi, bbr, bbi)
            bs[d].append(jnp.concatenate([xr, xi], axis=1))
            yr, yi = _cmul(pr, pi, cre, cim)
            cs[d].append(jnp.concatenate([yr, -yi], axis=1))
            if j == CHUNK:
                a_ref[d, k] = jnp.concatenate([pr, pi], axis=1)
            pr, pi = _cmul(pr, pi, ar, ai)
        toks = [_pos_tok(p, k) for p in range(CHUNK)]
        for p, s in enumerate(toks):
            j = CHUNK - 1 - s if d == 0 else s
            wst_ref[d, k, p * PART:(p + 1) * PART, :] = bs[d][j].astype(BF16)
        wout_t = jnp.concatenate([cs[d][t + 1 if d == 0 else CHUNK - t] for t in toks], axis=0)
        wout_ref[d, k] = wout_t.T.astype(BF16)
    kf = _dot_nt(bs[0][0], jnp.concatenate([cs[0][j] for j in range(CHUNK)], axis=0))
    kb = _dot_nt(bs[1][0], jnp.concatenate([cs[1][CHUNK - 1 - j] for j in range(CHUNK)], axis=0))
    ri = lax.broadcasted_iota(jnp.int32, (PART, PW), 0)
    ci = lax.broadcasted_iota(jnp.int32, (PART, PW), 1)
    for s in range(CHUNK):
        f = kf if s == 0 else pltpu.roll(kf, s * PART, 1)
        b = kb if s == CHUNK - 1 else pltpu.roll(kb, (s + 1) * PART, 1)
        blk = (jnp.where(ci >= s * PART, f, 0.0) + jnp.where(ci < (s + 1) * PART, b, 0.0)
               + jnp.where(ci == ri + s * PART, d_ref[k], 0.0))
        if k:
            blk = jnp.concatenate([pltpu.roll(blk[:, u * LANES:(u + 1) * LANES], k * PART, 1)
                                   for u in range(PW // LANES)], axis=1)
        p = _tok_pos(s, k)
        m_ref[k, p * PART:(p + 1) * PART, :] = blk.astype(BF16)


N_PREP_IN, N_PREP_OUT = 8, 4


def _s5_prep_body(*refs, n_cast):
    ins, refs = refs[:N_PREP_IN], refs[N_PREP_IN:]
    cast_in, refs = refs[:n_cast], refs[n_cast:]
    outs, cast_out = refs[:N_PREP_OUT], refs[N_PREP_OUT:]
    for k in range(PARTS_PER_SLAB):
        _s5_prep_part(k, *ins, *outs)
    for src, dst in zip(cast_in, cast_out):
        dst[...] = src[...].astype(BF16)


def _s5_prep(lam_re, lam_im, log_step, b_re, b_im, c_re, c_im, d, cast=()):
    gpp = PART // S5_GROUP
    eye = jnp.eye(gpp, dtype=F32)

    def state_rows(v):
        return v.reshape(2, N_PART, 1, PSTATE).transpose(1, 0, 2, 3)

    def block_diag(v):
        v = v.transpose(1, 0, 2, 3, 4)
        v = v[:, :, :, :, None, :] * eye[None, None, :, None, :, None]
        return v.reshape(N_PART, 2, PART, PSTATE)

    ls = jnp.broadcast_to(log_step[:, :, None], lam_re.shape)
    bshape = (2, N_PART, gpp, S5_STATE, S5_GROUP)
    cshape = (2, N_PART, gpp, S5_GROUP, S5_STATE)
    args = (state_rows(lam_re), state_rows(lam_im), state_rows(ls),
            block_diag(b_re.reshape(bshape).transpose(0, 1, 2, 4, 3)), block_diag(b_im.reshape(bshape).transpose(0, 1, 2, 4, 3)),
            block_diag(c_re.reshape(cshape)), block_diag(c_im.reshape(cshape)),
            jnp.tile(d.reshape(N_PART, 1, PART), (1, 1, CHUNK)))
    pps = PARTS_PER_SLAB
    vec = pl.BlockSpec((pps, 2, 1, PSTATE), lambda c: (c, 0, 0, 0))
    mat = pl.BlockSpec((pps, 2, PART, PSTATE), lambda c: (c, 0, 0, 0))
    slabs = [(w.shape[2] // N_SLAB, w.shape[3]) for w in cast]
    assert all(w.shape[2] % (N_SLAB * BF16_ROWS) == 0 for w in cast)
    wst, wout, m, a, *own = pl.pallas_call(
        functools.partial(_s5_prep_body, n_cast=len(cast)),
        grid=(N_SLAB,),
        in_specs=[vec, vec, vec, mat, mat, mat, mat, pl.BlockSpec((pps, 1, PW), lambda c: (c, 0, 0))]
                 + [pl.BlockSpec((None, None) + blk, lambda c: (0, 0, c, 0)) for blk in slabs],
        out_specs=[pl.BlockSpec((2, pps, PW, PCOLS), lambda c: (0, c, 0, 0)),
                   pl.BlockSpec((2, pps, PCOLS, PW), lambda c: (0, c, 0, 0)),
                   pl.BlockSpec((pps, PW, PW), lambda c: (c, 0, 0)),
                   pl.BlockSpec((2, pps, 1, PCOLS), lambda c: (0, c, 0, 0))]
                  + [pl.BlockSpec(blk, lambda c: (c, 0)) for blk in slabs],
        out_shape=[jax.ShapeDtypeStruct((2, N_PART, PW, PCOLS), BF16),
                   jax.ShapeDtypeStruct((2, N_PART, PCOLS, PW), BF16),
                   jax.ShapeDtypeStruct((N_PART, PW, PW), BF16),
                   jax.ShapeDtypeStruct((2, N_PART, 1, PCOLS), F32)]
                  + [jax.ShapeDtypeStruct(w.shape[2:], BF16) for w in cast],
        compiler_params=_params(("parallel",)),
    )(*args, *cast)
    return (wst, wout, m, a.reshape(2, N_SET, SUBLANES, PCOLS)), tuple(own)


def _s5_states_body(xf_ref, xb_ref, wst_ref, a_ref, of_ref, ob_ref, *scr, blocks_per_seq, rows):
    x_refs, o_refs = (xf_ref, xb_ref), (of_ref, ob_ref)
    halves = PCOLS // LANES
    n_piece = N_SET * halves
    car_scr = scr[2 * n_piece]
    piece = lambda d, st, ri: scr[d * n_piece + halves * st + ri]

    @pl.when(pl.program_id(0) % blocks_per_seq == 0)
    def _():
        car_scr[...] = jnp.zeros_like(car_scr)

    for d in range(2):
        for q in range(N_PART):
            s_loc = jnp.dot(x_refs[d][q], wst_ref[d, q], preferred_element_type=F32)
            for ri in range(halves):
                piece(d, q // SUBLANES, ri)[pl.ds(q % SUBLANES, rows, stride=SUBLANES), :] = (
                    s_loc[:, ri * LANES:(ri + 1) * LANES])

    def step(n, carry):
        new = []
        for d in range(2):
            r = n if d == 0 else rows - 1 - n
            r0 = pl.multiple_of(r * SUBLANES, SUBLANES)
            for st in range(N_SET):
                sr, si = carry[(d * N_SET + st) * 2], carry[(d * N_SET + st) * 2 + 1]
                ar, ai = a_ref[d, st, :, :LANES], a_ref[d, st, :, LANES:]
                re_scr, im_scr = piece(d, st, 0), piece(d, st, 1)
                xr, xi = re_scr[pl.ds(r0, SUBLANES), :], im_scr[pl.ds(r0, SUBLANES), :]
                re_scr[pl.ds(r0, SUBLANES), :] = sr
                im_scr[pl.ds(r0, SUBLANES), :] = si
                new += [ar * sr - ai * si + xr, ar * si + ai * sr + xi]
        return tuple(new)

    n_car = 2 * n_piece
    carry = lax.fori_loop(0, rows, step, tuple(car_scr[k] for k in range(n_car)), unroll=4)
    for k in range(n_car):
        car_scr[k] = carry[k]
    for d in range(2):
        for q in range(N_PART):
            for ri in range(halves):
                o_refs[d][q, :, ri * LANES:(ri + 1) * LANES] = (
                    piece(d, q // SUBLANES, ri)[pl.ds(q % SUBLANES, rows, stride=SUBLANES), :].astype(BF16))


def _s5_states(hs, wst, a, rows_per_seq):
    nrows = hs.shape[1]
    rows = min(ROW_BLOCK, rows_per_seq)
    nb = nrows // rows
    n_piece = N_SET * (PCOLS // LANES)
    return pl.pallas_call(
        functools.partial(_s5_states_body, blocks_per_seq=rows_per_seq // rows, rows=rows),
        grid=(nb,),
        in_specs=[pl.BlockSpec((N_PART, rows, PW), lambda i: (0, i, 0)),
                  pl.BlockSpec((N_PART, rows, PW), lambda i: (0, nb - 1 - i, 0)),
                  _pick((2, N_PART, PW, PCOLS)), _pick((2, N_SET, SUBLANES, PCOLS))],
        out_specs=[pl.BlockSpec((N_PART, rows, PCOLS), lambda i: (0, i, 0)),
                   pl.BlockSpec((N_PART, rows, PCOLS), lambda i: (0, nb - 1 - i, 0))],
        out_shape=[jax.ShapeDtypeStruct((N_PART, nrows, PCOLS), BF16)] * 2,
        scratch_shapes=[pltpu.VMEM((rows * SUBLANES, LANES), F32)] * (2 * n_piece)
                       + [pltpu.VMEM((2 * n_piece, SUBLANES, LANES), F32)],
        compiler_params=_params(("arbitrary",)),
    )(hs, hs, wst, a)


def _s5_out_body(x_ref, sf_ref, sb_ref, m_ref, wout_ref, o_ref, *, rows):
    for c in range(N_SLAB):
        ys = []
        for k in range(PARTS_PER_SLAB):
            q = PARTS_PER_SLAB * c + k
            y = jnp.dot(x_ref[q], m_ref[q], preferred_element_type=F32)
            y += jnp.dot(sf_ref[q], wout_ref[0, q], preferred_element_type=F32)
            y += jnp.dot(sb_ref[q], wout_ref[1, q], preferred_element_type=F32)
            ys.append(y)
        for u in range(CHUNK // PARTS_PER_SLAB):
            tok = _parts_to_tok([y[:, u * LANES:(u + 1) * LANES] for y in ys])
            for m, piece in enumerate(tok):
                o_ref[c, pl.ds(PARTS_PER_SLAB * u + m, rows, stride=CHUNK), :] = piece


def _s5_out(hs, sf, sb, m, wout, rows_per_seq):
    nrows = hs.shape[1]
    rows = min(ROW_BLOCK, rows_per_seq)
    blk = lambda i: (0, i, 0)
    return pl.pallas_call(
        functools.partial(_s5_out_body, rows=rows),
        grid=(nrows // rows,),
        in_specs=[pl.BlockSpec((N_PART, rows, PW), blk), pl.BlockSpec((N_PART, rows, PCOLS), blk),
                  pl.BlockSpec((N_PART, rows, PCOLS), blk),
                  _pick((N_PART, PW, PW)), _pick((2, N_PART, PCOLS, PW))],
        out_specs=pl.BlockSpec((N_SLAB, rows * CHUNK, SLAB), blk),
        out_shape=jax.ShapeDtypeStruct((N_SLAB, nrows * CHUNK, SLAB), F32),
        compiler_params=_params(("parallel",)),
    )(hs, sf, sb, m, wout)


def _trunk(x, seq_len, norm_g, final_norm_g, w16, s5, pool_scale, first=None):
    depth = norm_g.shape[0]
    gain = lambda layer, k: _pick((1, D_MODEL), layer, k)

    def ffn(x, g_spec, idx, **kw):
        nonlocal w16
        if w16 is not None:
            return _ffn(x, g_spec, idx, norm_g, *w16[:3], seq_len=seq_len, **kw)
        own, stacks = first
        out = _ffn(x, g_spec, (), norm_g, *own, seq_len=seq_len,
                   cast=tuple(w.reshape(-1, w.shape[-1]) for w in stacks), **kw)
        w16 = tuple(w.reshape(f.shape) for w, f in zip(out[-len(stacks):], stacks))
        return out[:-len(stacks)]

    for layer in range(depth):
        j = layer // 2
        last = dict(post="final", g2=final_norm_g, g2_spec=_pick((1, D_MODEL))) if layer == depth - 1 else {}
        if layer % 2 == 0:
            wst, wout, m, a = s5[j]
            x, hs = ffn(x, gain(layer, 0), (layer, 0), post="rows", g2=norm_g, g2_spec=gain(layer, 1))
            sf, sb = _s5_states(hs, wst, a, seq_len // CHUNK)
            y = _s5_out(hs, sf, sb, m, wout, seq_len // CHUNK)
            x = ffn(x, gain(layer, 2), (layer, 1), pre="glu", pre_args=(y, w16[3], w16[4], j), **last)[0]
        else:
            x = ffn(x, gain(layer, 0), (layer, 0))[0]
            x = ffn(x, gain(layer, 2), (layer, 1), pre="pool", pre_args=(gain(layer, 1), w16[5], pool_scale, j), **last)[0]
    return x, w16


def kernel(x_prompt, x_sample, norm_g, final_norm_g, ffn_w_gate, ffn_w_up, ffn_w_down, s5_lambda_re, s5_lambda_im, s5_log_step, s5_b_re, s5_b_im, s5_c_re, s5_c_im, s5_d, s5_w_glu_a, s5_w_glu_b, pool_w, pool_scale):
    stacks = (ffn_w_gate, ffn_w_up, ffn_w_down, s5_w_glu_a, s5_w_glu_b, pool_w)
    norm_g = norm_g.reshape(norm_g.shape[0], 3, 1, D_MODEL)
    final_norm_g = final_norm_g.reshape(1, D_MODEL)
    pool_scale = pool_scale.reshape(-1, 1, D_MODEL)
    s5, own = [], ()
    for j in range(s5_lambda_re.shape[0]):
        mats, own_j = _s5_prep(s5_lambda_re[j], s5_lambda_im[j], s5_log_step[j], s5_b_re[j], s5_b_im[j], s5_c_re[j],
                               s5_c_im[j], s5_d[j], cast=stacks[:3] if j == 0 else ())
        s5.append(mats)
        own = own or own_j
    first = (own, stacks)
    outs, w16 = [], None
    for x in (x_prompt, x_sample):
        bsz, seq, _ = x.shape
        y, w16 = _trunk(x.reshape(bsz * seq, D_MODEL), seq, norm_g, final_norm_g, w16, s5, pool_scale, first)
        outs.append(y.reshape(bsz, seq, D_MODEL))
    return tuple(outs)
```

```python
import functools
import itertools

import jax
import jax.numpy as jnp
from jax import lax
from jax.experimental import pallas as pl
from jax.experimental.pallas import tpu as pltpu

D_MODEL = 1024
D_FF = 2816
RMS_EPS = 1e-6
S5_GROUP = 16
S5_STATE = 64
POOL_WINDOWS = (2, 4, 8, 16)
POOL_CH = D_MODEL // len(POOL_WINDOWS)
POOL_HALO = 8

LANES = 128
SUBLANES = 8
BF16_ROWS = 16
MXU_COLS = 256
SLAB = LANES
N_SLAB = D_MODEL // SLAB
PART = 32
PARTS_PER_SLAB = SLAB // PART
N_PART = D_MODEL // PART
PSTATE = (PART // S5_GROUP) * S5_STATE
PCOLS = 2 * PSTATE
CHUNK = SUBLANES
PW = CHUNK * PART
N_SET = N_PART // SUBLANES
ROW_BLOCK = 128
TOKEN_TILE = 512
VMEM_LIMIT = 56 * 1024 * 1024

F32 = jnp.float32
BF16 = jnp.bfloat16


def _params(sem, vmem=VMEM_LIMIT):
    return pltpu.CompilerParams(dimension_semantics=sem, vmem_limit_bytes=vmem)


def _pick(tail, *lead):
    tail, lead = tuple(tail), tuple(lead)
    return pl.BlockSpec((None,) * len(lead) + tail, lambda *_: lead + (0,) * len(tail), pipeline_mode=pl.Buffered(1))


def _rms(x, g):
    return x * lax.rsqrt(jnp.mean(x * x, axis=-1, keepdims=True) + RMS_EPS) * g


def _tok_pos(t, k):
    return PARTS_PER_SLAB * (t // PARTS_PER_SLAB) + (t % PARTS_PER_SLAB + k) % PARTS_PER_SLAB


def _pos_tok(p, k):
    return PARTS_PER_SLAB * (p // PARTS_PER_SLAB) + (p % PARTS_PER_SLAB - k) % PARTS_PER_SLAB


def _pick_blocks(src, first):
    n = PARTS_PER_SLAB
    blk = lax.broadcasted_iota(jnp.int32, src[0].shape, 1) // PART
    out = src[(first + n - 1) % n]
    for p in range(n - 2, -1, -1):
        out = jnp.where(blk == p, src[(first + p) % n], out)
    return out


def _tok_to_parts(tok):
    n = PARTS_PER_SLAB
    rolled = [tok[0]] + [pltpu.roll(tok[m], m * PART, 1) for m in range(1, n)]
    return [_pick_blocks(rolled, -k) for k in range(n)]


def _parts_to_tok(parts):
    n = PARTS_PER_SLAB
    tok = []
    for m in range(n):
        z = _pick_blocks(parts, -m)
        tok.append(z if m == 0 else pltpu.roll(z, (n - m) * PART, 1))
    return tok


def _pool_pieces(x_ref, prev_ref, next_ref, g, w_ref, scale_ref, e_scr, inv_scr, o_ref, j, tiles_per_seq, seq_len):
    lo, n = 2 * POOL_HALO, TOKEN_TILE + 4 * POOL_HALO
    pad = jnp.zeros((POOL_HALO, D_MODEL), F32)
    e_scr[...] = jnp.concatenate([pad, jnp.where(j == 0, 0.0, _rms(prev_ref[...], g)), _rms(x_ref[...], g),
                                  jnp.where(j == tiles_per_seq - 1, 0.0, _rms(next_ref[...], g)), pad], axis=0)
    pos = j * TOKEN_TILE + lax.broadcasted_iota(jnp.int32, (TOKEN_TILE, 1), 0)
    for gi, win in enumerate(POOL_WINDOWS):
        cnt = jnp.minimum(pos + (win - win // 2), seq_len) - jnp.maximum(pos - win // 2, 0)
        inv_scr[gi] = jnp.broadcast_to(1.0 / cnt.astype(F32), (TOKEN_TILE, LANES))
    yield
    for gi, win in enumerate(POOL_WINDOWS):
        ps = []
        for half in range(POOL_CH // LANES):
            cols = slice(gi * POOL_CH + half * LANES, gi * POOL_CH + (half + 1) * LANES)
            c = e_scr[:, cols]
            w = 1
            while w < win:
                c = c + pltpu.roll(c, w, 0)
                w *= 2
            lead = win - win // 2 - 1
            if lead:
                c = pltpu.roll(c, n - lead, 0)
            ps.append((c[lo:lo + TOKEN_TILE] * inv_scr[gi] - e_scr[lo:lo + TOKEN_TILE, cols]).astype(BF16))
            yield
        cols = slice(gi * POOL_CH, (gi + 1) * POOL_CH)
        z = jnp.dot(jnp.concatenate(ps, axis=1), w_ref[gi], preferred_element_type=F32)
        o_ref[:, cols] = x_ref[:, cols] + z * scale_ref[:, cols]
        yield


def _ffn_core(x, g_ref, wg_ref, wu_ref, wd_ref, between=None):
    h = _rms(x, g_ref[...]).astype(BF16)
    gate = jnp.dot(h, wg_ref[...], preferred_element_type=F32)
    if between is not None:
        between()
    up = jnp.dot(h, wu_ref[...], preferred_element_type=F32)
    act = (gate * jax.nn.sigmoid(gate) * up).astype(BF16)
    return x + 0.5 * jnp.dot(act, wd_ref[...], preferred_element_type=F32)


def _ffn_core_with(side, x_ref, g_ref, wg_ref, wu_ref, wd_ref):
    h = _rms(x_ref[...], g_ref[...]).astype(BF16)
    acts = []
    for c in range(D_FF // MXU_COLS):
        cols = slice(c * MXU_COLS, (c + 1) * MXU_COLS)
        gate = jnp.dot(h, wg_ref[:, cols], preferred_element_type=F32)
        up = jnp.dot(h, wu_ref[:, cols], preferred_element_type=F32)
        acts.append((gate * jax.nn.sigmoid(gate) * up).astype(BF16))
        next(side, None)
    act = jnp.concatenate(acts, axis=1)
    ys = []
    for c in range(D_MODEL // MXU_COLS):
        cols = slice(c * MXU_COLS, (c + 1) * MXU_COLS)
        ys.append(x_ref[:, cols] + 0.5 * jnp.dot(act, wd_ref[:, cols], preferred_element_type=F32))
        next(side, None)
    for _ in side:
        pass
    return jnp.concatenate(ys, axis=1)


def _rows_pieces(h_scr, hs_ref):
    rows = TOKEN_TILE // CHUNK
    for c in range(N_SLAB):
        for u in range(CHUNK // PARTS_PER_SLAB):
            tok = [h_scr[c, pl.ds(PARTS_PER_SLAB * u + m, rows, stride=CHUNK), :] for m in range(PARTS_PER_SLAB)]
            for k, piece in enumerate(_tok_to_parts(tok)):
                hs_ref[PARTS_PER_SLAB * c + k, :, u * LANES:(u + 1) * LANES] = piece.astype(BF16)
        yield


def _by_parity(i, n, first, stage, last, buf0, buf1):
    @pl.when(i == 0)
    def _():
        if first is None:
            buf1[...] = jnp.zeros_like(buf1)
        else:
            first(buf0)

    staged = (i < n) if last is not None else (i <= n)
    even = staged & (i % 2 == 0)
    if first is not None:
        even = even & (i > 0)

    @pl.when(even)
    def _():
        stage(buf0, buf1)

    @pl.when(staged & (i % 2 == 1))
    def _():
        stage(buf1, buf0)

    if last is not None:
        @pl.when(i == n)
        def _():
            last(buf1 if n % 2 == 0 else buf0)


def _ffn_body(*refs, pre, post, n_cast, n_tiles, tiles_per_seq, seq_len):
    it = iter(refs)
    x_ref = next(it)
    if pre == "glu":
        y_ref, wa_ref, wb_ref = next(it), next(it), next(it)
    if pre == "pool":
        prev_ref, next_ref, pg_ref, pw_ref, ps_ref = next(it), next(it), next(it), next(it), next(it)
    g_ref, wg_ref, wu_ref, wd_ref = next(it), next(it), next(it), next(it)
    if post is not None:
        g2_ref = next(it)
    cast_in = [next(it) for _ in range(n_cast)]
    o_ref = next(it)
    if post == "rows":
        hs_ref = next(it)
    cast_out = [next(it) for _ in range(n_cast)]
    if pre == "pool":
        buf0, buf1, e_scr, inv_scr = next(it), next(it), next(it), next(it)
    if post == "rows":
        buf0, buf1 = next(it), next(it)
    i = pl.program_id(0)

    def cast_slabs():
        for src, dst in zip(cast_in, cast_out):
            dst[...] = src[...].astype(BF16)

    ffn = functools.partial(_ffn_core, g_ref=g_ref, wg_ref=wg_ref, wu_ref=wu_ref, wd_ref=wd_ref,
                            between=cast_slabs if n_cast else None)

    if pre == "pool":
        def pool(mixed_w):
            j = jnp.minimum(i, n_tiles - 1) % tiles_per_seq
            return _pool_pieces(x_ref, prev_ref, next_ref, pg_ref[...], pw_ref, ps_ref, e_scr, inv_scr, mixed_w,
                                j, tiles_per_seq, seq_len)

        def first(mixed_w):
            for _ in pool(mixed_w):
                pass

        def stage(mixed_w, mixed_r):
            y = _ffn_core_with(pool(mixed_w), mixed_r, g_ref, wg_ref, wu_ref, wd_ref)
            o_ref[...] = _rms(y, g2_ref[...]) if post == "final" else y
        _by_parity(i, n_tiles, first, stage, None, buf0, buf1)
        return

    def x_in():
        x = x_ref[...]
        if pre == "glu":
            y = jnp.concatenate([y_ref[c] for c in range(N_SLAB)], axis=1)
            gy = jax.nn.gelu(y).astype(BF16)
            a = jnp.dot(gy, wa_ref[...], preferred_element_type=F32)
            b = jnp.dot(gy, wb_ref[...], preferred_element_type=F32)
            x = x + a * jax.nn.sigmoid(b)
        return x

    if post == "rows":
        def cast_piece():
            cast_slabs()
            yield

        def stage(h_w, h_r):
            side = itertools.chain(cast_piece(), _rows_pieces(h_r, hs_ref))
            y = _ffn_core_with(side, x_ref if pre is None else x_in(), g_ref, wg_ref, wu_ref, wd_ref)
            o_ref[...] = y
            h2 = _rms(y, g2_ref[...])
            for c in range(N_SLAB):
                h_w[c] = h2[:, c * SLAB:(c + 1) * SLAB]

        def last(h_r):
            for _ in _rows_pieces(h_r, hs_ref):
                pass
        _by_parity(i, n_tiles, None, stage, last, buf0, buf1)
        return
    y = ffn(x_in())
    o_ref[...] = _rms(y, g2_ref[...]) if post == "final" else y


def _ffn(x, g_spec, w_idx, norm_g, wg, wu, wd, *, seq_len, pre=None, pre_args=(), post=None, g2=None, g2_spec=None,
         cast=()):
    assert not (pre == "pool" and (post == "rows" or cast)), "the skewed pool stage has no extra outputs"
    n = x.shape[0]
    n_tiles = n // TOKEN_TILE
    skewed = pre == "pool" or post == "rows"
    cur = (lambda i: jnp.minimum(i, n_tiles - 1)) if skewed else (lambda i: i)
    behind = lambda i: jnp.maximum(i - 1, 0)
    tile_in = pl.BlockSpec((TOKEN_TILE, D_MODEL), lambda i: (cur(i), 0))
    tile_out = pl.BlockSpec((TOKEN_TILE, D_MODEL), (lambda i: (behind(i), 0)) if pre == "pool" else (lambda i: (cur(i), 0)))
    args, specs, scratch = [x], [tile_in], []
    if pre == "glu":
        y, wa, wb, j = pre_args
        args += [y, wa, wb]
        specs += [pl.BlockSpec((N_SLAB, TOKEN_TILE, SLAB), lambda i: (0, cur(i), 0)),
                  _pick((D_MODEL, D_MODEL), j), _pick((D_MODEL, D_MODEL), j)]
    if pre == "pool":
        pg_spec, pw, ps, j = pre_args
        halo_blocks = TOKEN_TILE // POOL_HALO
        last = n // POOL_HALO - 1
        args += [x, x, norm_g, pw, ps]
        specs += [pl.BlockSpec((POOL_HALO, D_MODEL), lambda i: (jnp.maximum(cur(i) * halo_blocks - 1, 0), 0)),
                  pl.BlockSpec((POOL_HALO, D_MODEL), lambda i: (jnp.minimum((cur(i) + 1) * halo_blocks, last), 0)),
                  pg_spec, _pick((len(POOL_WINDOWS), POOL_CH, POOL_CH), j), _pick((1, D_MODEL), j)]
        scratch += [pltpu.VMEM((TOKEN_TILE, D_MODEL), F32)] * 2
        scratch += [pltpu.VMEM((TOKEN_TILE + 4 * POOL_HALO, D_MODEL), F32),
                    pltpu.VMEM((len(POOL_WINDOWS), TOKEN_TILE, LANES), F32)]
    args += [norm_g, wg, wu, wd]
    specs += [g_spec, _pick((D_MODEL, D_FF), *w_idx), _pick((D_MODEL, D_FF), *w_idx), _pick((D_FF, D_MODEL), *w_idx)]
    if post is not None:
        args.append(g2)
        specs.append(g2_spec)
    out_shape, out_specs = [jax.ShapeDtypeStruct((n, D_MODEL), F32)], [tile_out]
    if post == "rows":
        rows = TOKEN_TILE // CHUNK
        out_shape.append(jax.ShapeDtypeStruct((N_PART, n // CHUNK, PW), BF16))
        out_specs.append(pl.BlockSpec((N_PART, rows, PW), lambda i: (0, behind(i), 0)))
        scratch += [pltpu.VMEM((N_SLAB, TOKEN_TILE, SLAB), F32)] * 2
    for w in cast:
        slab = pl.BlockSpec((w.shape[0] // n_tiles, w.shape[1]), lambda i: (cur(i), 0))
        assert w.shape[0] % (n_tiles * BF16_ROWS) == 0
        args.append(w)
        specs.append(slab)
        out_shape.append(jax.ShapeDtypeStruct(w.shape, BF16))
        out_specs.append(slab)
    return pl.pallas_call(
        functools.partial(_ffn_body, pre=pre, post=post, n_cast=len(cast), n_tiles=n_tiles,
                          tiles_per_seq=seq_len // TOKEN_TILE, seq_len=seq_len),
        grid=(n_tiles + 1 if skewed else n_tiles,),
        in_specs=specs, out_specs=out_specs, out_shape=out_shape, scratch_shapes=scratch,
        compiler_params=_params(("arbitrary",) if skewed else ("parallel",)),
    )(*args)


def _cmul(ar, ai, br, bi):
    return ar * br - ai * bi, ar * bi + ai * br


def _dot_nt(a, b):
    return lax.dot_general(a, b, (((1,), (1,)), ((), ())), precision=lax.Precision.HIGHEST, preferred_element_type=F32)


def _s5_prep_part(k, lr_ref, li_ref, ls_ref, br_ref, bi_ref, cr_ref, ci_ref, d_ref, wst_ref, wout_ref, m_ref, a_ref):
    bs, cs = [], []
    for d in range(2):
        lr, li = lr_ref[k, d], li_ref[k, d]
        step = jnp.exp(ls_ref[k, d])
        mag = jnp.exp(lr * step)
        ar, ai = mag * jnp.cos(li * step), mag * jnp.sin(li * step)
        den = lr * lr + li * li
        nr = ar - 1.0
        kr, ki = (nr * lr + ai * li) / den, (ai * lr - nr * li) / den
        bbr, bbi = _cmul(kr, ki, br_ref[k, d], bi_ref[k, d])
        cre, cim = cr_ref[k, d], ci_ref[k, d]
        pr, pi = jnp.ones_like(ar), jnp.zeros_like(ar)
        bs.append([])
        cs.append([])
        for j in range(CHUNK + 1):
            xr, xi = _cmul(pr, pi, bbr, bbi)
            bs[d].append(jnp.concatenate([xr, xi], axis=1))
            yr, yi = _cmul(pr, pi, cre, cim)
            cs[d].append(jnp.concatenate([yr, -yi], axis=1))
            if j == CHUNK:
                a_ref[d, k] = jnp.concatenate([pr, pi], axis=1)
            pr, pi = _cmul(pr, pi, ar, ai)
        toks = [_pos_tok(p, k) for p in range(CHUNK)]
        for p, s in enumerate(toks):
            j = CHUNK - 1 - s if d == 0 else s
            wst_ref[d, k, p * PART:(p + 1) * PART, :] = bs[d][j].astype(BF16)
        wout_t = jnp.concatenate([cs[d][t + 1 if d == 0 else CHUNK - t] for t in toks], axis=0)
        wout_ref[d, k] = wout_t.T.astype(BF16)
    kf = _dot_nt(bs[0][0], jnp.concatenate([cs[0][j] for j in range(CHUNK)], axis=0))
    kb = _dot_nt(bs[1][0], jnp.concatenate([cs[1][CHUNK - 1 - j] for j in range(CHUNK)], axis=0))
    ri = lax.broadcasted_iota(jnp.int32, (PART, PW), 0)
    ci = lax.broadcasted_iota(jnp.int32, (PART, PW), 1)
    for s in range(CHUNK):
        f = kf if s == 0 else pltpu.roll(kf, s * PART, 1)
        b = kb if s == CHUNK - 1 else pltpu.roll(kb, (s + 1) * PART, 1)
        blk = (jnp.where(ci >= s * PART, f, 0.0) + jnp.where(ci < (s + 1) * PART, b, 0.0)
               + jnp.where(ci == ri + s * PART, d_ref[k], 0.0))
        if k:
            blk = jnp.concatenate([pltpu.roll(blk[:, u * LANES:(u + 1) * LANES], k * PART, 1)
                                   for u in range(PW // LANES)], axis=1)
        p = _tok_pos(s, k)
        m_ref[k, p * PART:(p + 1) * PART, :] = blk.astype(BF16)


N_PREP_IN, N_PREP_OUT = 8, 4


def _s5_prep_body(*refs, n_cast):
    ins, refs = refs[:N_PREP_IN], refs[N_PREP_IN:]
    cast_in, refs = refs[:n_cast], refs[n_cast:]
    outs, cast_out = refs[:N_PREP_OUT], refs[N_PREP_OUT:]
    for k in range(PARTS_PER_SLAB):
        _s5_prep_part(k, *ins, *outs)
    for src, dst in zip(cast_in, cast_out):
        dst[...] = src[...].astype(BF16)


def _s5_prep(lam_re, lam_im, log_step, b_re, b_im, c_re, c_im, d, cast=()):
    gpp = PART // S5_GROUP
    eye = jnp.eye(gpp, dtype=F32)

    def state_rows(v):
        return v.reshape(2, N_PART, 1, PSTATE).transpose(1, 0, 2, 3)

    def block_diag(v):
        v = v.transpose(1, 0, 2, 3, 4)
        v = v[:, :, :, :, None, :] * eye[None, None, :, None, :, None]
        return v.reshape(N_PART, 2, PART, PSTATE)

    ls = jnp.broadcast_to(log_step[:, :, None], lam_re.shape)
    bshape = (2, N_PART, gpp, S5_STATE, S5_GROUP)
    cshape = (2, N_PART, gpp, S5_GROUP, S5_STATE)
    args = (state_rows(lam_re), state_rows(lam_im), state_rows(ls),
            block_diag(b_re.reshape(bshape).transpose(0, 1, 2, 4, 3)), block_diag(b_im.reshape(bshape).transpose(0, 1, 2, 4, 3)),
            block_diag(c_re.reshape(cshape)), block_diag(c_im.reshape(cshape)),
            jnp.tile(d.reshape(N_PART, 1, PART), (1, 1, CHUNK)))
    pps = PARTS_PER_SLAB
    vec = pl.BlockSpec((pps, 2, 1, PSTATE), lambda c: (c, 0, 0, 0))
    mat = pl.BlockSpec((pps, 2, PART, PSTATE), lambda c: (c, 0, 0, 0))
    slabs = [(w.shape[2] // N_SLAB, w.shape[3]) for w in cast]
    assert all(w.shape[2] % (N_SLAB * BF16_ROWS) == 0 for w in cast)
    wst, wout, m, a, *own = pl.pallas_call(
        functools.partial(_s5_prep_body, n_cast=len(cast)),
        grid=(N_SLAB,),
        in_specs=[vec, vec, vec, mat, mat, mat, mat, pl.BlockSpec((pps, 1, PW), lambda c: (c, 0, 0))]
                 + [pl.BlockSpec((None, None) + blk, lambda c: (0, 0, c, 0)) for blk in slabs],
        out_specs=[pl.BlockSpec((2, pps, PW, PCOLS), lambda c: (0, c, 0, 0)),
                   pl.BlockSpec((2, pps, PCOLS, PW), lambda c: (0, c, 0, 0)),
                   pl.BlockSpec((pps, PW, PW), lambda c: (c, 0, 0)),
                   pl.BlockSpec((2, pps, 1, PCOLS), lambda c: (0, c, 0, 0))]
                  + [pl.BlockSpec(blk, lambda c: (c, 0)) for blk in slabs],
        out_shape=[jax.ShapeDtypeStruct((2, N_PART, PW, PCOLS), BF16),
                   jax.ShapeDtypeStruct((2, N_PART, PCOLS, PW), BF16),
                   jax.ShapeDtypeStruct((N_PART, PW, PW), BF16),
                   jax.ShapeDtypeStruct((2, N_PART, 1, PCOLS), F32)]
                  + [jax.ShapeDtypeStruct(w.shape[2:], BF16) for w in cast],
        compiler_params=_params(("parallel",)),
    )(*args, *cast)
    return (wst, wout, m, a.reshape(2, N_SET, SUBLANES, PCOLS)), tuple(own)


def _s5_states_body(xf_ref, xb_ref, wst_ref, a_ref, of_ref, ob_ref, *scr, blocks_per_seq, rows):
    x_refs, o_refs = (xf_ref, xb_ref), (of_ref, ob_ref)
    halves = PCOLS // LANES
    n_piece = N_SET * halves
    car_scr = scr[2 * n_piece]
    piece = lambda d, st, ri: scr[d * n_piece + halves * st + ri]

    @pl.when(pl.program_id(0) % blocks_per_seq == 0)
    def _():
        car_scr[...] = jnp.zeros_like(car_scr)

    for d in range(2):
        for q in range(N_PART):
            s_loc = jnp.dot(x_refs[d][q], wst_ref[d, q], preferred_element_type=F32)
            for ri in range(halves):
                piece(d, q // SUBLANES, ri)[pl.ds(q % SUBLANES, rows, stride=SUBLANES), :] = (
                    s_loc[:, ri * LANES:(ri + 1) * LANES])

    def step(n, carry):
        new = []
        for d in range(2):
            r = n if d == 0 else rows - 1 - n
            r0 = pl.multiple_of(r * SUBLANES, SUBLANES)
            for st in range(N_SET):
                sr, si = carry[(d * N_SET + st) * 2], carry[(d * N_SET + st) * 2 + 1]
                ar, ai = a_ref[d, st, :, :LANES], a_ref[d, st, :, LANES:]
                re_scr, im_scr = piece(d, st, 0), piece(d, st, 1)
                xr, xi = re_scr[pl.ds(r0, SUBLANES), :], im_scr[pl.ds(r0, SUBLANES), :]
                re_scr[pl.ds(r0, SUBLANES), :] = sr
                im_scr[pl.ds(r0, SUBLANES), :] = si
                new += [ar * sr - ai * si + xr, ar * si + ai * sr + xi]
        return tuple(new)

    n_car = 2 * n_piece
    carry = lax.fori_loop(0, rows, step, tuple(car_scr[k] for k in range(n_car)), unroll=4)
    for k in range(n_car):
        car_scr[k] = carry[k]
    for d in range(2):
        for q in range(N_PART):
            for ri in range(halves):
                o_refs[d][q, :, ri * LANES:(ri + 1) * LANES] = (
                    piece(d, q // SUBLANES, ri)[pl.ds(q % SUBLANES, rows, stride=SUBLANES), :].astype(BF16))


def _s5_states(hs, wst, a, rows_per_seq):
    nrows = hs.shape[1]
    rows = min(ROW_BLOCK, rows_per_seq)
    nb = nrows // rows
    n_piece = N_SET * (PCOLS // LANES)
    return pl.pallas_call(
        functools.partial(_s5_states_body, blocks_per_seq=rows_per_seq // rows, rows=rows),
        grid=(nb,),
        in_specs=[pl.BlockSpec((N_PART, rows, PW), lambda i: (0, i, 0)),
                  pl.BlockSpec((N_PART, rows, PW), lambda i: (0, nb - 1 - i, 0)),
                  _pick((2, N_PART, PW, PCOLS)), _pick((2, N_SET, SUBLANES, PCOLS))],
        out_specs=[pl.BlockSpec((N_PART, rows, PCOLS), lambda i: (0, i, 0)),
                   pl.BlockSpec((N_PART, rows, PCOLS), lambda i: (0, nb - 1 - i, 0))],
        out_shape=[jax.ShapeDtypeStruct((N_PART, nrows, PCOLS), BF16)] * 2,
        scratch_shapes=[pltpu.VMEM((rows * SUBLANES, LANES), F32)] * (2 * n_piece)
                       + [pltpu.VMEM((2 * n_piece, SUBLANES, LANES), F32)],
        compiler_params=_params(("arbitrary",)),
    )(hs, hs, wst, a)


def _s5_out_body(x_ref, sf_ref, sb_ref, m_ref, wout_ref, o_ref, *, rows):
    for c in range(N_SLAB):
        ys = []
        for k in range(PARTS_PER_SLAB):
            q = PARTS_PER_SLAB * c + k
            y = jnp.dot(x_ref[q], m_ref[q], preferred_element_type=F32)
            y += jnp.dot(sf_ref[q], wout_ref[0, q], preferred_element_type=F32)
            y += jnp.dot(sb_ref[q], wout_ref[1, q], preferred_element_type=F32)
            ys.append(y)
        for u in range(CHUNK // PARTS_PER_SLAB):
            tok = _parts_to_tok([y[:, u * LANES:(u + 1) * LANES] for y in ys])
            for m, piece in enumerate(tok):
                o_ref[c, pl.ds(PARTS_PER_SLAB * u + m, rows, stride=CHUNK), :] = piece


def _s5_out(hs, sf, sb, m, wout, rows_per_seq):
    nrows = hs.shape[1]
    rows = min(2 * ROW_BLOCK, rows_per_seq)
    blk = lambda i: (0, i, 0)
    return pl.pallas_call(
        functools.partial(_s5_out_body, rows=rows),
        grid=(nrows // rows,),
        in_specs=[pl.BlockSpec((N_PART, rows, PW), blk), pl.BlockSpec((N_PART, rows, PCOLS), blk),
                  pl.BlockSpec((N_PART, rows, PCOLS), blk),
                  _pick((N_PART, PW, PW)), _pick((2, N_PART, PCOLS, PW))],
        out_specs=pl.BlockSpec((N_SLAB, rows * CHUNK, SLAB), blk),
        out_shape=jax.ShapeDtypeStruct((N_SLAB, nrows * CHUNK, SLAB), F32),
        compiler_params=_params(("parallel",)),
    )(hs, sf, sb, m, wout)


def _trunk(x, seq_len, norm_g, final_norm_g, w16, s5, pool_scale, first=None):
    depth = norm_g.shape[0]
    gain = lambda layer, k: _pick((1, D_MODEL), layer, k)

    def ffn(x, g_spec, idx, **kw):
        nonlocal w16
        if w16 is not None:
            return _ffn(x, g_spec, idx, norm_g, *w16[:3], seq_len=seq_len, **kw)
        own, stacks = first
        out = _ffn(x, g_spec, (), norm_g, *own, seq_len=seq_len,
                   cast=tuple(w.reshape(-1, w.shape[-1]) for w in stacks), **kw)
        w16 = tuple(w.reshape(f.shape) for w, f in zip(out[-len(stacks):], stacks))
        return out[:-len(stacks)]

    for layer in range(depth):
        j = layer // 2
        last = dict(post="final", g2=final_norm_g, g2_spec=_pick((1, D_MODEL))) if layer == depth - 1 else {}
        if layer % 2 == 0:
            wst, wout, m, a = s5[j]
            x, hs = ffn(x, gain(layer, 0), (layer, 0), post="rows", g2=norm_g, g2_spec=gain(layer, 1))
            sf, sb = _s5_states(hs, wst, a, seq_len // CHUNK)
            y = _s5_out(hs, sf, sb, m, wout, seq_len // CHUNK)
            x = ffn(x, gain(layer, 2), (layer, 1), pre="glu", pre_args=(y, w16[3], w16[4], j), **last)[0]
        else:
            x = ffn(x, gain(layer, 0), (layer, 0))[0]
            x = ffn(x, gain(layer, 2), (layer, 1), pre="pool", pre_args=(gain(layer, 1), w16[5], pool_scale, j), **last)[0]
    return x, w16


def kernel(x_prompt, x_sample, norm_g, final_norm_g, ffn_w_gate, ffn_w_up, ffn_w_down, s5_lambda_re, s5_lambda_im, s5_log_step, s5_b_re, s5_b_im, s5_c_re, s5_c_im, s5_d, s5_w_glu_a, s5_w_glu_b, pool_w, pool_scale):
    stacks = (ffn_w_gate, ffn_w_up, ffn_w_down, s5_w_glu_a, s5_w_glu_b, pool_w)
    norm_g = norm_g.reshape(norm_g.shape[0], 3, 1, D_MODEL)
    final_norm_g = final_norm_g.reshape(1, D_MODEL)
    pool_scale = pool_scale.reshape(-1, 1, D_MODEL)
    s5, own = [], ()
    for j in range(s5_lambda_re.shape[0]):
        mats, own_j = _s5_prep(s5_lambda_re[j], s5_lambda_im[j], s5_log_step[j], s5_b_re[j], s5_b_im[j], s5_c_re[j],
                               s5_c_im[j], s5_d[j], cast=stacks[:3] if j == 0 else ())
        s5.append(mats)
        own = own or own_j
    first = (own, stacks)
    outs, w16 = [], None
    for x in (x_prompt, x_sample):
        bsz, seq, _ = x.shape
        y, w16 = _trunk(x.reshape(bsz * seq, D_MODEL), seq, norm_g, final_norm_g, w16, s5, pool_scale, first)
        outs.append(y.reshape(bsz, seq, D_MODEL))
    return tuple(outs)
```

```python
import functools
import itertools

import jax
import jax.numpy as jnp
from jax import lax
from jax.experimental import pallas as pl
from jax.experimental.pallas import tpu as pltpu

D_MODEL = 1024
D_FF = 2816
RMS_EPS = 1e-6
S5_GROUP = 16
S5_STATE = 64
POOL_WINDOWS = (2, 4, 8, 16)
POOL_CH = D_MODEL // len(POOL_WINDOWS)
POOL_HALO = 8

LANES = 128
SUBLANES = 8
BF16_ROWS = 16
MXU_COLS = 256
SLAB = LANES
N_SLAB = D_MODEL // SLAB
PART = 32
PARTS_PER_SLAB = SLAB // PART
N_PART = D_MODEL // PART
PSTATE = (PART // S5_GROUP) * S5_STATE
PCOLS = 2 * PSTATE
CHUNK = SUBLANES
PW = CHUNK * PART
N_SET = N_PART // SUBLANES
ROW_BLOCK = 128
TOKEN_TILE = 512
VMEM_LIMIT = 56 * 1024 * 1024

F32 = jnp.float32
BF16 = jnp.bfloat16


def _params(sem, vmem=VMEM_LIMIT):
    return pltpu.CompilerParams(dimension_semantics=sem, vmem_limit_bytes=vmem)


def _pick(tail, *lead):
    tail, lead = tuple(tail), tuple(lead)
    return pl.BlockSpec((None,) * len(lead) + tail, lambda *_: lead + (0,) * len(tail), pipeline_mode=pl.Buffered(1))


def _rms(x, g):
    return x * lax.rsqrt(jnp.mean(x * x, axis=-1, keepdims=True) + RMS_EPS) * g


def _tok_pos(t, k):
    return PARTS_PER_SLAB * (t // PARTS_PER_SLAB) + (t % PARTS_PER_SLAB + k) % PARTS_PER_SLAB


def _pos_tok(p, k):
    return PARTS_PER_SLAB * (p // PARTS_PER_SLAB) + (p % PARTS_PER_SLAB - k) % PARTS_PER_SLAB


def _pick_blocks(src, first):
    n = PARTS_PER_SLAB
    blk = lax.broadcasted_iota(jnp.int32, src[0].shape, 1) // PART
    out = src[(first + n - 1) % n]
    for p in range(n - 2, -1, -1):
        out = jnp.where(blk == p, src[(first + p) % n], out)
    return out


def _tok_to_parts(tok):
    n = PARTS_PER_SLAB
    rolled = [tok[0]] + [pltpu.roll(tok[m], m * PART, 1) for m in range(1, n)]
    return [_pick_blocks(rolled, -k) for k in range(n)]


def _parts_to_tok(parts):
    n = PARTS_PER_SLAB
    tok = []
    for m in range(n):
        z = _pick_blocks(parts, -m)
        tok.append(z if m == 0 else pltpu.roll(z, (n - m) * PART, 1))
    return tok


def _pool_pieces(x_ref, prev_ref, next_ref, g, w_ref, scale_ref, e_scr, inv_scr, o_ref, j, tiles_per_seq, seq_len):
    lo, n = 2 * POOL_HALO, TOKEN_TILE + 4 * POOL_HALO
    pad = jnp.zeros((POOL_HALO, D_MODEL), F32)
    e_scr[...] = jnp.concatenate([pad, jnp.where(j == 0, 0.0, _rms(prev_ref[...], g)), _rms(x_ref[...], g),
                                  jnp.where(j == tiles_per_seq - 1, 0.0, _rms(next_ref[...], g)), pad], axis=0)
    pos = j * TOKEN_TILE + lax.broadcasted_iota(jnp.int32, (TOKEN_TILE, 1), 0)
    for gi, win in enumerate(POOL_WINDOWS):
        cnt = jnp.minimum(pos + (win - win // 2), seq_len) - jnp.maximum(pos - win // 2, 0)
        inv_scr[gi] = jnp.broadcast_to(1.0 / cnt.astype(F32), (TOKEN_TILE, LANES))
    yield
    for gi, win in enumerate(POOL_WINDOWS):
        ps = []
        for half in range(POOL_CH // LANES):
            cols = slice(gi * POOL_CH + half * LANES, gi * POOL_CH + (half + 1) * LANES)
            c = e_scr[:, cols]
            w = 1
            while w < win:
                c = c + pltpu.roll(c, w, 0)
                w *= 2
            lead = win - win // 2 - 1
            if lead:
                c = pltpu.roll(c, n - lead, 0)
            ps.append((c[lo:lo + TOKEN_TILE] * inv_scr[gi] - e_scr[lo:lo + TOKEN_TILE, cols]).astype(BF16))
            yield
        cols = slice(gi * POOL_CH, (gi + 1) * POOL_CH)
        z = jnp.dot(jnp.concatenate(ps, axis=1), w_ref[gi], preferred_element_type=F32)
        o_ref[:, cols] = x_ref[:, cols] + z * scale_ref[:, cols]
        yield


def _ffn_core(x, g_ref, wg_ref, wu_ref, wd_ref, between=None):
    h = _rms(x, g_ref[...]).astype(BF16)
    gate = jnp.dot(h, wg_ref[...], preferred_element_type=F32)
    if between is not None:
        between()
    up = jnp.dot(h, wu_ref[...], preferred_element_type=F32)
    act = (gate * jax.nn.sigmoid(gate) * up).astype(BF16)
    return x + 0.5 * jnp.dot(act, wd_ref[...], preferred_element_type=F32)


def _ffn_core_with(side, x_ref, g_ref, wg_ref, wu_ref, wd_ref):
    h = _rms(x_ref[...], g_ref[...]).astype(BF16)
    acts = []
    for c in range(D_FF // MXU_COLS):
        cols = slice(c * MXU_COLS, (c + 1) * MXU_COLS)
        gate = jnp.dot(h, wg_ref[:, cols], preferred_element_type=F32)
        up = jnp.dot(h, wu_ref[:, cols], preferred_element_type=F32)
        acts.append((gate * jax.nn.sigmoid(gate) * up).astype(BF16))
        next(side, None)
    act = jnp.concatenate(acts, axis=1)
    ys = []
    for c in range(D_MODEL // MXU_COLS):
        cols = slice(c * MXU_COLS, (c + 1) * MXU_COLS)
        ys.append(x_ref[:, cols] + 0.5 * jnp.dot(act, wd_ref[:, cols], preferred_element_type=F32))
        next(side, None)
    for _ in side:
        pass
    return jnp.concatenate(ys, axis=1)


def _rows_pieces(h_scr, hs_ref):
    rows = TOKEN_TILE // CHUNK
    for c in range(N_SLAB):
        for u in range(CHUNK // PARTS_PER_SLAB):
            tok = [h_scr[c, pl.ds(PARTS_PER_SLAB * u + m, rows, stride=CHUNK), :] for m in range(PARTS_PER_SLAB)]
            for k, piece in enumerate(_tok_to_parts(tok)):
                hs_ref[PARTS_PER_SLAB * c + k, :, u * LANES:(u + 1) * LANES] = piece.astype(BF16)
        yield


def _by_parity(i, n, first, stage, last, buf0, buf1):
    @pl.when(i == 0)
    def _():
        if first is None:
            buf1[...] = jnp.zeros_like(buf1)
        else:
            first(buf0)

    staged = (i < n) if last is not None else (i <= n)
    even = staged & (i % 2 == 0)
    if first is not None:
        even = even & (i > 0)

    @pl.when(even)
    def _():
        stage(buf0, buf1)

    @pl.when(staged & (i % 2 == 1))
    def _():
        stage(buf1, buf0)

    if last is not None:
        @pl.when(i == n)
        def _():
            last(buf1 if n % 2 == 0 else buf0)


def _ffn_body(*refs, pre, post, n_cast, n_tiles, tiles_per_seq, seq_len):
    it = iter(refs)
    x_ref = next(it)
    if pre == "glu":
        y_ref, wa_ref, wb_ref = next(it), next(it), next(it)
    if pre == "pool":
        prev_ref, next_ref, pg_ref, pw_ref, ps_ref = next(it), next(it), next(it), next(it), next(it)
    g_ref, wg_ref, wu_ref, wd_ref = next(it), next(it), next(it), next(it)
    if post is not None:
        g2_ref = next(it)
    cast_in = [next(it) for _ in range(n_cast)]
    o_ref = next(it)
    if post == "rows":
        hs_ref = next(it)
    cast_out = [next(it) for _ in range(n_cast)]
    if pre == "pool":
        buf0, buf1, e_scr, inv_scr = next(it), next(it), next(it), next(it)
    if post == "rows":
        buf0, buf1 = next(it), next(it)
    i = pl.program_id(0)

    def cast_slabs():
        for src, dst in zip(cast_in, cast_out):
            dst[...] = src[...].astype(BF16)

    ffn = functools.partial(_ffn_core, g_ref=g_ref, wg_ref=wg_ref, wu_ref=wu_ref, wd_ref=wd_ref,
                            between=cast_slabs if n_cast else None)

    if pre == "pool":
        def pool(mixed_w):
            j = jnp.minimum(i, n_tiles - 1) % tiles_per_seq
            return _pool_pieces(x_ref, prev_ref, next_ref, pg_ref[...], pw_ref, ps_ref, e_scr, inv_scr, mixed_w,
                                j, tiles_per_seq, seq_len)

        def first(mixed_w):
            for _ in pool(mixed_w):
                pass

        def stage(mixed_w, mixed_r):
            y = _ffn_core_with(pool(mixed_w), mixed_r, g_ref, wg_ref, wu_ref, wd_ref)
            o_ref[...] = _rms(y, g2_ref[...]) if post == "final" else y
        _by_parity(i, n_tiles, first, stage, None, buf0, buf1)
        return

    def x_in():
        x = x_ref[...]
        if pre == "glu":
            y = jnp.concatenate([y_ref[c] for c in range(N_SLAB)], axis=1)
            gy = jax.nn.gelu(y).astype(BF16)
            a = jnp.dot(gy, wa_ref[...], preferred_element_type=F32)
            b = jnp.dot(gy, wb_ref[...], preferred_element_type=F32)
            x = x + a * jax.nn.sigmoid(b)
        return x

    if post == "rows":
        def cast_piece():
            cast_slabs()
            yield

        def stage(h_w, h_r):
            side = itertools.chain(cast_piece(), _rows_pieces(h_r, hs_ref))
            y = _ffn_core_with(side, x_ref if pre is None else x_in(), g_ref, wg_ref, wu_ref, wd_ref)
            o_ref[...] = y
            h2 = _rms(y, g2_ref[...])
            for c in range(N_SLAB):
                h_w[c] = h2[:, c * SLAB:(c + 1) * SLAB]

        def last(h_r):
            for _ in _rows_pieces(h_r, hs_ref):
                pass
        _by_parity(i, n_tiles, None, stage, last, buf0, buf1)
        return
    y = ffn(x_in())
    o_ref[...] = _rms(y, g2_ref[...]) if post == "final" else y


def _ffn(x, g_spec, w_idx, norm_g, wg, wu, wd, *, seq_len, pre=None, pre_args=(), post=None, g2=None, g2_spec=None,
         cast=()):
    assert not (pre == "pool" and (post == "rows" or cast)), "the skewed pool stage has no extra outputs"
    n = x.shape[0]
    n_tiles = n // TOKEN_TILE
    skewed = pre == "pool" or post == "rows"
    cur = (lambda i: jnp.minimum(i, n_tiles - 1)) if skewed else (lambda i: i)
    behind = lambda i: jnp.maximum(i - 1, 0)
    tile_in = pl.BlockSpec((TOKEN_TILE, D_MODEL), lambda i: (cur(i), 0))
    tile_out = pl.BlockSpec((TOKEN_TILE, D_MODEL), (lambda i: (behind(i), 0)) if pre == "pool" else (lambda i: (cur(i), 0)))
    args, specs, scratch = [x], [tile_in], []
    if pre == "glu":
        y, wa, wb, j = pre_args
        args += [y, wa, wb]
        specs += [pl.BlockSpec((N_SLAB, TOKEN_TILE, SLAB), lambda i: (0, cur(i), 0)),
                  _pick((D_MODEL, D_MODEL), j), _pick((D_MODEL, D_MODEL), j)]
    if pre == "pool":
        pg_spec, pw, ps, j = pre_args
        halo_blocks = TOKEN_TILE // POOL_HALO
        last = n // POOL_HALO - 1
        args += [x, x, norm_g, pw, ps]
        specs += [pl.BlockSpec((POOL_HALO, D_MODEL), lambda i: (jnp.maximum(cur(i) * halo_blocks - 1, 0), 0)),
                  pl.BlockSpec((POOL_HALO, D_MODEL), lambda i: (jnp.minimum((cur(i) + 1) * halo_blocks, last), 0)),
                  pg_spec, _pick((len(POOL_WINDOWS), POOL_CH, POOL_CH), j), _pick((1, D_MODEL), j)]
        scratch += [pltpu.VMEM((TOKEN_TILE, D_MODEL), F32)] * 2
        scratch += [pltpu.VMEM((TOKEN_TILE + 4 * POOL_HALO, D_MODEL), F32),
                    pltpu.VMEM((len(POOL_WINDOWS), TOKEN_TILE, LANES), F32)]
    args += [norm_g, wg, wu, wd]
    specs += [g_spec, _pick((D_MODEL, D_FF), *w_idx), _pick((D_MODEL, D_FF), *w_idx), _pick((D_FF, D_MODEL), *w_idx)]
    if post is not None:
        args.append(g2)
        specs.append(g2_spec)
    out_shape, out_specs = [jax.ShapeDtypeStruct((n, D_MODEL), F32)], [tile_out]
    if post == "rows":
        rows = TOKEN_TILE // CHUNK
        out_shape.append(jax.ShapeDtypeStruct((N_PART, n // CHUNK, PW), BF16))
        out_specs.append(pl.BlockSpec((N_PART, rows, PW), lambda i: (0, behind(i), 0)))
        scratch += [pltpu.VMEM((N_SLAB, TOKEN_TILE, SLAB), F32)] * 2
    for w in cast:
        slab = pl.BlockSpec((w.shape[0] // n_tiles, w.shape[1]), lambda i: (cur(i), 0))
        assert w.shape[0] % (n_tiles * BF16_ROWS) == 0
        args.append(w)
        specs.append(slab)
        out_shape.append(jax.ShapeDtypeStruct(w.shape, BF16))
        out_specs.append(slab)
    return pl.pallas_call(
        functools.partial(_ffn_body, pre=pre, post=post, n_cast=len(cast), n_tiles=n_tiles,
                          tiles_per_seq=seq_len // TOKEN_TILE, seq_len=seq_len),
        grid=(n_tiles + 1 if skewed else n_tiles,),
        in_specs=specs, out_specs=out_specs, out_shape=out_shape, scratch_shapes=scratch,
        compiler_params=_params(("arbitrary",) if skewed else ("parallel",)),
    )(*args)


def _cmul(ar, ai, br, bi):
    return ar * br - ai * bi, ar * bi + ai * br


def _dot_nt(a, b):
    return lax.dot_general(a, b, (((1,), (1,)), ((), ())), precision=lax.Precision.HIGHEST, preferred_element_type=F32)


def _s5_prep_part(k, lr_ref, li_ref, ls_ref, br_ref, bi_ref, cr_ref, ci_ref, d_ref, wst_ref, wout_ref, m_ref, a_ref):
    bs, cs = [], []
    for d in range(2):
        lr, li = lr_ref[k, d], li_ref[k, d]
        step = jnp.exp(ls_ref[k, d])
        mag = jnp.exp(lr * step)
        ar, ai = mag * jnp.cos(li * step), mag * jnp.sin(li * step)
        den = lr * lr + li * li
        nr = ar - 1.0
        kr, ki = (nr * lr + ai * li) / den, (ai * lr - nr * li) / den
        bbr, bbi = _cmul(kr, ki, br_ref[k, d], bi_ref[k, d])
        cre, cim = cr_ref[k, d], ci_ref[k, d]
        pr, pi = jnp.ones_like(ar), jnp.zeros_like(ar)
        bs.append([])
        cs.append([])
        for j in range(CHUNK + 1):
            xr, xi = _cmul(pr, pi, bbr, bbi)
            bs[d].append(jnp.concatenate([xr, xi], axis=1))
            yr, yi = _cmul(pr, pi, cre, cim)
            cs[d].append(jnp.concatenate([yr, -yi], axis=1))
            if j == CHUNK:
                a_ref[d, k] = jnp.concatenate([pr, pi], axis=1)
            pr, pi = _cmul(pr, pi, ar, ai)
        toks = [_pos_tok(p, k) for p in range(CHUNK)]
        for p, s in enumerate(toks):
            j = CHUNK - 1 - s if d == 0 else s
            wst_ref[d, k, p * PART:(p + 1) * PART, :] = bs[d][j].astype(BF16)
        wout_t = jnp.concatenate([cs[d][t + 1 if d == 0 else CHUNK - t] for t in toks], axis=0)
        wout_ref[d, k] = wout_t.T.astype(BF16)
    kf = _dot_nt(bs[0][0], jnp.concatenate([cs[0][j] for j in range(CHUNK)], axis=0))
    kb = _dot_nt(bs[1][0], jnp.concatenate([cs[1][CHUNK - 1 - j] for j in range(CHUNK)], axis=0))
    ri = lax.broadcasted_iota(jnp.int32, (PART, PW), 0)
    ci = lax.broadcasted_iota(jnp.int32, (PART, PW), 1)
    for s in range(CHUNK):
        f = kf if s == 0 else pltpu.roll(kf, s * PART, 1)
        b = kb if s == CHUNK - 1 else pltpu.roll(kb, (s + 1) * PART, 1)
        blk = (jnp.where(ci >= s * PART, f, 0.0) + jnp.where(ci < (s + 1) * PART, b, 0.0)
               + jnp.where(ci == ri + s * PART, d_ref[k], 0.0))
        if k:
            blk = jnp.concatenate([pltpu.roll(blk[:, u * LANES:(u + 1) * LANES], k * PART, 1)
                                   for u in range(PW // LANES)], axis=1)
        p = _tok_pos(s, k)
        m_ref[k, p * PART:(p + 1) * PART, :] = blk.astype(BF16)


N_PREP_IN, N_PREP_OUT = 8, 4


def _s5_prep_body(*refs, n_cast):
    ins, refs = refs[:N_PREP_IN], refs[N_PREP_IN:]
    cast_in, refs = refs[:n_cast], refs[n_cast:]
    outs, cast_out = refs[:N_PREP_OUT], refs[N_PREP_OUT:]
    for k in range(PARTS_PER_SLAB):
        _s5_prep_part(k, *ins, *outs)
    for src, dst in zip(cast_in, cast_out):
        dst[...] = src[...].astype(BF16)


def _s5_prep(lam_re, lam_im, log_step, b_re, b_im, c_re, c_im, d, cast=()):
    gpp = PART // S5_GROUP
    eye = jnp.eye(gpp, dtype=F32)

    def state_rows(v):
        return v.reshape(2, N_PART, 1, PSTATE).transpose(1, 0, 2, 3)

    def block_diag(v):
        v = v.transpose(1, 0, 2, 3, 4)
        v = v[:, :, :, :, None, :] * eye[None, None, :, None, :, None]
        return v.reshape(N_PART, 2, PART, PSTATE)

    ls = jnp.broadcast_to(log_step[:, :, None], lam_re.shape)
    bshape = (2, N_PART, gpp, S5_STATE, S5_GROUP)
    cshape = (2, N_PART, gpp, S5_GROUP, S5_STATE)
    args = (state_rows(lam_re), state_rows(lam_im), state_rows(ls),
            block_diag(b_re.reshape(bshape).transpose(0, 1, 2, 4, 3)), block_diag(b_im.reshape(bshape).transpose(0, 1, 2, 4, 3)),
            block_diag(c_re.reshape(cshape)), block_diag(c_im.reshape(cshape)),
            jnp.tile(d.reshape(N_PART, 1, PART), (1, 1, CHUNK)))
    pps = PARTS_PER_SLAB
    vec = pl.BlockSpec((pps, 2, 1, PSTATE), lambda c: (c, 0, 0, 0))
    mat = pl.BlockSpec((pps, 2, PART, PSTATE), lambda c: (c, 0, 0, 0))
    slabs = [(w.shape[2] // N_SLAB, w.shape[3]) for w in cast]
    assert all(w.shape[2] % (N_SLAB * BF16_ROWS) == 0 for w in cast)
    wst, wout, m, a, *own = pl.pallas_call(
        functools.partial(_s5_prep_body, n_cast=len(cast)),
        grid=(N_SLAB,),
        in_specs=[vec, vec, vec, mat, mat, mat, mat, pl.BlockSpec((pps, 1, PW), lambda c: (c, 0, 0))]
                 + [pl.BlockSpec((None, None) + blk, lambda c: (0, 0, c, 0)) for blk in slabs],
        out_specs=[pl.BlockSpec((2, pps, PW, PCOLS), lambda c: (0, c, 0, 0)),
                   pl.BlockSpec((2, pps, PCOLS, PW), lambda c: (0, c, 0, 0)),
                   pl.BlockSpec((pps, PW, PW), lambda c: (c, 0, 0)),
                   pl.BlockSpec((2, pps, 1, PCOLS), lambda c: (0, c, 0, 0))]
                  + [pl.BlockSpec(blk, lambda c: (c, 0)) for blk in slabs],
        out_shape=[jax.ShapeDtypeStruct((2, N_PART, PW, PCOLS), BF16),
                   jax.ShapeDtypeStruct((2, N_PART, PCOLS, PW), BF16),
                   jax.ShapeDtypeStruct((N_PART, PW, PW), BF16),
                   jax.ShapeDtypeStruct((2, N_PART, 1, PCOLS), F32)]
                  + [jax.ShapeDtypeStruct(w.shape[2:], BF16) for w in cast],
        compiler_params=_params(("parallel",)),
    )(*args, *cast)
    return (wst, wout, m, a.reshape(2, N_SET, SUBLANES, PCOLS)), tuple(own)


def _s5_states_body(xf_ref, xb_ref, wst_ref, a_ref, of_ref, ob_ref, *scr, blocks_per_seq, rows):
    x_refs, o_refs = (xf_ref, xb_ref), (of_ref, ob_ref)
    halves = PCOLS // LANES
    n_piece = N_SET * halves
    car_scr = scr[2 * n_piece]
    piece = lambda d, st, ri: scr[d * n_piece + halves * st + ri]

    @pl.when(pl.program_id(0) % blocks_per_seq == 0)
    def _():
        car_scr[...] = jnp.zeros_like(car_scr)

    for d in range(2):
        for q in range(N_PART):
            s_loc = jnp.dot(x_refs[d][q], wst_ref[d, q], preferred_element_type=F32)
            for ri in range(halves):
                piece(d, q // SUBLANES, ri)[pl.ds(q % SUBLANES, rows, stride=SUBLANES), :] = (
                    s_loc[:, ri * LANES:(ri + 1) * LANES])

    def step(n, carry):
        new = []
        for d in range(2):
            r = n if d == 0 else rows - 1 - n
            r0 = pl.multiple_of(r * SUBLANES, SUBLANES)
            for st in range(N_SET):
                sr, si = carry[(d * N_SET + st) * 2], carry[(d * N_SET + st) * 2 + 1]
                ar, ai = a_ref[d, st, :, :LANES], a_ref[d, st, :, LANES:]
                re_scr, im_scr = piece(d, st, 0), piece(d, st, 1)
                xr, xi = re_scr[pl.ds(r0, SUBLANES), :], im_scr[pl.ds(r0, SUBLANES), :]
                re_scr[pl.ds(r0, SUBLANES), :] = sr
                im_scr[pl.ds(r0, SUBLANES), :] = si
                new += [ar * sr - ai * si + xr, ar * si + ai * sr + xi]
        return tuple(new)

    n_car = 2 * n_piece
    carry = lax.fori_loop(0, rows, step, tuple(car_scr[k] for k in range(n_car)), unroll=4)
    for k in range(n_car):
        car_scr[k] = carry[k]
    for d in range(2):
        for q in range(N_PART):
            for ri in range(halves):
                o_refs[d][q, :, ri * LANES:(ri + 1) * LANES] = (
                    piece(d, q // SUBLANES, ri)[pl.ds(q % SUBLANES, rows, stride=SUBLANES), :].astype(BF16))


def _s5_states(hs, wst, a, rows_per_seq):
    nrows = hs.shape[1]
    rows = min(ROW_BLOCK, rows_per_seq)
    nb = nrows // rows
    n_piece = N_SET * (PCOLS // LANES)
    return pl.pallas_call(
        functools.partial(_s5_states_body, blocks_per_seq=rows_per_seq // rows, rows=rows),
        grid=(nb,),
        in_specs=[pl.BlockSpec((N_PART, rows, PW), lambda i: (0, i, 0)),
                  pl.BlockSpec((N_PART, rows, PW), lambda i: (0, nb - 1 - i, 0)),
                  _pick((2, N_PART, PW, PCOLS)), _pick((2, N_SET, SUBLANES, PCOLS))],
        out_specs=[pl.BlockSpec((N_PART, rows, PCOLS), lambda i: (0, i, 0)),
                   pl.BlockSpec((N_PART, rows, PCOLS), lambda i: (0, nb - 1 - i, 0))],
        out_shape=[jax.ShapeDtypeStruct((N_PART, nrows, PCOLS), BF16)] * 2,
        scratch_shapes=[pltpu.VMEM((rows * SUBLANES, LANES), F32)] * (2 * n_piece)
                       + [pltpu.VMEM((2 * n_piece, SUBLANES, LANES), F32)],
        compiler_params=_params(("arbitrary",)),
    )(hs, hs, wst, a)


def _s5_out_body(x_ref, sf_ref, sb_ref, w_ref, o_ref, *, rows):
    for c in range(N_SLAB):
        ys = []
        for k in range(PARTS_PER_SLAB):
            q = PARTS_PER_SLAB * c + k
            lhs = jnp.concatenate([x_ref[q], sf_ref[q], sb_ref[q]], axis=1)
            y = jnp.dot(lhs, w_ref[q], preferred_element_type=F32)
            ys.append(y)
        for u in range(CHUNK // PARTS_PER_SLAB):
            tok = _parts_to_tok([y[:, u * LANES:(u + 1) * LANES] for y in ys])
            for m, piece in enumerate(tok):
                o_ref[c, pl.ds(PARTS_PER_SLAB * u + m, rows, stride=CHUNK), :] = piece


def _s5_out(hs, sf, sb, m, wout, rows_per_seq):
    nrows = hs.shape[1]
    rows = min(2 * ROW_BLOCK, rows_per_seq)
    blk = lambda i: (0, i, 0)
    w = jnp.concatenate([m, wout[0], wout[1]], axis=1)
    return pl.pallas_call(
        functools.partial(_s5_out_body, rows=rows),
        grid=(nrows // rows,),
        in_specs=[pl.BlockSpec((N_PART, rows, PW), blk), pl.BlockSpec((N_PART, rows, PCOLS), blk),
                  pl.BlockSpec((N_PART, rows, PCOLS), blk),
                  _pick((N_PART, PW + 2 * PCOLS, PW))],
        out_specs=pl.BlockSpec((N_SLAB, rows * CHUNK, SLAB), blk),
        out_shape=jax.ShapeDtypeStruct((N_SLAB, nrows * CHUNK, SLAB), F32),
        compiler_params=_params(("parallel",)),
    )(hs, sf, sb, w)


def _trunk(x, seq_len, norm_g, final_norm_g, w16, s5, pool_scale, first=None):
    depth = norm_g.shape[0]
    gain = lambda layer, k: _pick((1, D_MODEL), layer, k)

    def ffn(x, g_spec, idx, **kw):
        nonlocal w16
        if w16 is not None:
            return _ffn(x, g_spec, idx, norm_g, *w16[:3], seq_len=seq_len, **kw)
        own, stacks = first
        out = _ffn(x, g_spec, (), norm_g, *own, seq_len=seq_len,
                   cast=tuple(w.reshape(-1, w.shape[-1]) for w in stacks), **kw)
        w16 = tuple(w.reshape(f.shape) for w, f in zip(out[-len(stacks):], stacks))
        return out[:-len(stacks)]

    for layer in range(depth):
        j = layer // 2
        last = dict(post="final", g2=final_norm_g, g2_spec=_pick((1, D_MODEL))) if layer == depth - 1 else {}
        if layer % 2 == 0:
            wst, wout, m, a = s5[j]
            x, hs = ffn(x, gain(layer, 0), (layer, 0), post="rows", g2=norm_g, g2_spec=gain(layer, 1))
            sf, sb = _s5_states(hs, wst, a, seq_len // CHUNK)
            y = _s5_out(hs, sf, sb, m, wout, seq_len // CHUNK)
            x = ffn(x, gain(layer, 2), (layer, 1), pre="glu", pre_args=(y, w16[3], w16[4], j), **last)[0]
        else:
            x = ffn(x, gain(layer, 0), (layer, 0))[0]
            x = ffn(x, gain(layer, 2), (layer, 1), pre="pool", pre_args=(gain(layer, 1), w16[5], pool_scale, j), **last)[0]
    return x, w16


def kernel(x_prompt, x_sample, norm_g, final_norm_g, ffn_w_gate, ffn_w_up, ffn_w_down, s5_lambda_re, s5_lambda_im, s5_log_step, s5_b_re, s5_b_im, s5_c_re, s5_c_im, s5_d, s5_w_glu_a, s5_w_glu_b, pool_w, pool_scale):
    stacks = (ffn_w_gate, ffn_w_up, ffn_w_down, s5_w_glu_a, s5_w_glu_b, pool_w)
    norm_g = norm_g.reshape(norm_g.shape[0], 3, 1, D_MODEL)
    final_norm_g = final_norm_g.reshape(1, D_MODEL)
    pool_scale = pool_scale.reshape(-1, 1, D_MODEL)
    s5, own = [], ()
    for j in range(s5_lambda_re.shape[0]):
        mats, own_j = _s5_prep(s5_lambda_re[j], s5_lambda_im[j], s5_log_step[j], s5_b_re[j], s5_b_im[j], s5_c_re[j],
                               s5_c_im[j], s5_d[j], cast=stacks[:3] if j == 0 else ())
        s5.append(mats)
        own = own or own_j
    first = (own, stacks)
    outs, w16 = [], None
    for x in (x_prompt, x_sample):
        bsz, seq, _ = x.shape
        y, w16 = _trunk(x.reshape(bsz * seq, D_MODEL), seq, norm_g, final_norm_g, w16, s5, pool_scale, first)
        outs.append(y.reshape(bsz, seq, D_MODEL))
    return tuple(outs)
```
